```python
import math
import jax, jax.numpy as jnp
from jax import lax
import numpy as np

D_MODEL = 2048
BATCH = 1
SEQ = 16384
DEPTH = 1

CHUNK = 64
Q_BLOCK = 128
ATTN_WIDTH = D_MODEL // 2
SSM_WIDTH = D_MODEL - ATTN_WIDTH
N_HEADS = 8
HEAD_DIM = ATTN_WIDTH // (2 * N_HEADS)
V_DIM = 2 * HEAD_DIM
ROT_DIM = HEAD_DIM // 4
ROPE_THETA = 500000.0
SSM_GROUP = 16
N_SSM_GROUPS = SSM_WIDTH // SSM_GROUP
SSM_STATE = 64
D_FF = 256 * ((8 * D_MODEL // 3 + 255) // 256)
IN_COLS = 3 * ATTN_WIDTH + SSM_WIDTH
N_MOD = 9
EPS = 1e-6
NEG_INF = -1e30

kernel_name = "hymba_diffattn_s5_macaron_block"


def rms_norm(x, g):
    xf = x.astype(jnp.float32)
    y = xf * lax.rsqrt(jnp.mean(xf * xf, axis=-1, keepdims=True) + EPS)
    return (y * g.astype(jnp.float32)).astype(x.dtype)


def modulate(h, shift, scale):
    return h * (1 + scale[:, None, :]) + shift[:, None, :]


def swiglu(h, w1, w3, w2):
    return (jax.nn.silu(h @ w1) * (h @ w3)) @ w2


def rope_partial(x, cos, sin):
    half = ROT_DIM // 2
    x1, x2, xp = x[..., :half], x[..., half:ROT_DIM], x[..., ROT_DIM:]
    c = cos[:, :, None, None, :]
    s = sin[:, :, None, None, :]
    rot = jnp.concatenate([x1 * c - x2 * s, x1 * s + x2 * c], axis=-1).astype(x.dtype)
    return jnp.concatenate([rot, xp], axis=-1)


def diff_attention(q, k, v, chunk_id, lam, lam_init, subln_g):
    B, L = q.shape[0], q.shape[1]
    nb = L // Q_BLOCK
    qb = q.reshape(B, nb, Q_BLOCK, N_HEADS, 2, HEAD_DIM).transpose(1, 0, 2, 3, 4, 5)
    cb = chunk_id.reshape(B, nb, Q_BLOCK).transpose(1, 0, 2)
    scale = HEAD_DIM ** -0.5

    def one_block(args):
        q_blk, c_blk = args
        s = jnp.einsum('bqhcd,bkhcd->bhcqk', q_blk, k).astype(jnp.float32) * scale
        mask = chunk_id[:, None, :] <= c_blk[:, :, None]
        s = jnp.where(mask[:, None, None, :, :], s, NEG_INF)
        p = jax.nn.softmax(s, axis=-1)
        a = p[:, :, 0] - lam * p[:, :, 1]
        return jnp.einsum('bhqk,bkhe->bqhe', a.astype(v.dtype), v)

    o = lax.map(one_block, (qb, cb))
    o = o.transpose(1, 0, 2, 3, 4).reshape(B, L, N_HEADS, V_DIM)
    o = rms_norm(o, subln_g) * (1.0 - lam_init)
    return o.reshape(B, L, ATTN_WIDTH)


def s5_ssm(u, a_re, a_im, log_dt, b_re, b_im, c_re, c_im, d_skip):
    f32 = jnp.float32
    B, L = u.shape[0], u.shape[1]
    ug = u.astype(f32).reshape(B, L, N_SSM_GROUPS, SSM_GROUP)
    lam = lax.complex(jnp.minimum(a_re.astype(f32), -1e-4), a_im.astype(f32))
    dt = jnp.exp(log_dt.astype(f32))[:, None]
    lam_bar = jnp.exp(lam * dt)
    b = lax.complex(b_re.astype(f32), b_im.astype(f32))
    b_bar = ((lam_bar - 1.0) / lam)[..., None] * b
    bu = jnp.einsum('gpc,blgc->blgp', b_bar, ug.astype(jnp.complex64))
    a_elems = jnp.broadcast_to(lam_bar, bu.shape)

    def combine(e1, e2):
        a1, s1 = e1
        a2, s2 = e2
        return a2 * a1, a2 * s1 + s2

    _, states = lax.associative_scan(combine, (a_elems, bu), axis=1)
    y = (jnp.einsum('gcp,blgp->blgc', c_re.astype(f32), jnp.real(states))
         - jnp.einsum('gcp,blgp->blgc', c_im.astype(f32), jnp.imag(states)))
    y = y + d_skip.astype(f32) * ug
    return y.reshape(B, L, SSM_WIDTH).astype(u.dtype)


def setup_inputs(seed: int = 0) -> dict:
    key = jax.random.key(seed)
    ks = iter(jax.random.split(key, 40))
    f32 = jnp.float32

    def nrm(shape, scale):
        return jax.random.normal(next(ks), shape, f32) * scale

    def gain(shape):
        return 1.0 + 0.02 * jax.random.normal(next(ks), shape, f32)

    x = nrm((BATCH, SEQ, D_MODEL), 1.0)
    c = nrm((BATCH, D_MODEL), 1.0)
    offset = jax.random.randint(next(ks), (BATCH, 1), 0, 64, dtype=jnp.int32) * CHUNK
    positions = (offset + jnp.arange(SEQ, dtype=jnp.int32)[None, :]).astype(jnp.int32)

    w_ada = nrm((DEPTH, D_MODEL, N_MOD * D_MODEL), D_MODEL ** -0.5)
    b_ada = nrm((DEPTH, N_MOD * D_MODEL), 0.02)

    ffn1_norm = gain((DEPTH, D_MODEL))
    ffn1_w1 = nrm((DEPTH, D_MODEL, D_FF), D_MODEL ** -0.5)
    ffn1_w3 = nrm((DEPTH, D_MODEL, D_FF), D_MODEL ** -0.5)
    ffn1_w2 = nrm((DEPTH, D_FF, D_MODEL), D_FF ** -0.5)

    mix_norm = gain((DEPTH, D_MODEL))
    w_in = nrm((DEPTH, D_MODEL, IN_COLS), D_MODEL ** -0.5)
    q_norm = gain((DEPTH, HEAD_DIM))
    k_norm = gain((DEPTH, HEAD_DIM))
    lambda_q1 = nrm((DEPTH, HEAD_DIM), 0.1)
    lambda_k1 = nrm((DEPTH, HEAD_DIM), 0.1)
    lambda_q2 = nrm((DEPTH, HEAD_DIM), 0.1)
    lambda_k2 = nrm((DEPTH, HEAD_DIM), 0.1)
    attn_subln = gain((DEPTH, V_DIM))

    n_idx = jnp.arange(SSM_STATE, dtype=f32)
    ssm_a_re = -0.5 + nrm((DEPTH, N_SSM_GROUPS, SSM_STATE), 0.01)
    ssm_a_im = math.pi * n_idx + nrm((DEPTH, N_SSM_GROUPS, SSM_STATE), 0.01)
    ssm_log_dt = jax.random.uniform(next(ks), (DEPTH, N_SSM_GROUPS), f32,
                                    math.log(1e-3), math.log(1e-1))
    ssm_b_re = nrm((DEPTH, N_SSM_GROUPS, SSM_STATE, SSM_GROUP), (2 * SSM_GROUP) ** -0.5)
    ssm_b_im = nrm((DEPTH, N_SSM_GROUPS, SSM_STATE, SSM_GROUP), (2 * SSM_GROUP) ** -0.5)
    ssm_c_re = nrm((DEPTH, N_SSM_GROUPS, SSM_GROUP, SSM_STATE), (2 * SSM_STATE) ** -0.5)
    ssm_c_im = nrm((DEPTH, N_SSM_GROUPS, SSM_GROUP, SSM_STATE), (2 * SSM_STATE) ** -0.5)
    ssm_d = nrm((DEPTH, N_SSM_GROUPS, SSM_GROUP), 1.0)
    w_glu = nrm((DEPTH, SSM_WIDTH, SSM_WIDTH), SSM_WIDTH ** -0.5)
    b_glu = nrm((DEPTH, SSM_WIDTH), 0.02)
    ssm_out_norm = gain((DEPTH, SSM_WIDTH))
    w_out = nrm((DEPTH, D_MODEL, D_MODEL), D_MODEL ** -0.5)

    ffn2_norm = gain((DEPTH, D_MODEL))
    ffn2_w1 = nrm((DEPTH, D_MODEL, D_FF), D_MODEL ** -0.5)
    ffn2_w3 = nrm((DEPTH, D_MODEL, D_FF), D_MODEL ** -0.5)
    ffn2_w2 = nrm((DEPTH, D_FF, D_MODEL), D_FF ** -0.5)

    return {"x": x, "c": c, "positions": positions, "w_ada": w_ada, "b_ada": b_ada,
            "ffn1_norm": ffn1_norm, "ffn1_w1": ffn1_w1, "ffn1_w3": ffn1_w3, "ffn1_w2": ffn1_w2,
            "mix_norm": mix_norm, "w_in": w_in, "q_norm": q_norm, "k_norm": k_norm,
            "lambda_q1": lambda_q1, "lambda_k1": lambda_k1, "lambda_q2": lambda_q2,
            "lambda_k2": lambda_k2, "attn_subln": attn_subln,
            "ssm_a_re": ssm_a_re, "ssm_a_im": ssm_a_im, "ssm_log_dt": ssm_log_dt,
            "ssm_b_re": ssm_b_re, "ssm_b_im": ssm_b_im, "ssm_c_re": ssm_c_re,
            "ssm_c_im": ssm_c_im, "ssm_d": ssm_d, "w_glu": w_glu, "b_glu": b_glu,
            "ssm_out_norm": ssm_out_norm, "w_out": w_out,
            "ffn2_norm": ffn2_norm, "ffn2_w1": ffn2_w1, "ffn2_w3": ffn2_w3, "ffn2_w2": ffn2_w2}


def reference(x, c, positions, w_ada, b_ada, ffn1_norm, ffn1_w1, ffn1_w3, ffn1_w2,
              mix_norm, w_in, q_norm, k_norm, lambda_q1, lambda_k1, lambda_q2, lambda_k2,
              attn_subln, ssm_a_re, ssm_a_im, ssm_log_dt, ssm_b_re, ssm_b_im, ssm_c_re,
              ssm_c_im, ssm_d, w_glu, b_glu, ssm_out_norm, w_out,
              ffn2_norm, ffn2_w1, ffn2_w3, ffn2_w2):
    f32 = jnp.float32
    B, L, _ = x.shape
    chunk_id = positions // CHUNK
    inv_freq = ROPE_THETA ** (-jnp.arange(0, ROT_DIM, 2, dtype=f32) / ROT_DIM)
    ang = positions.astype(f32)[..., None] * inv_freq
    cos, sin = jnp.cos(ang), jnp.sin(ang)
    cond = jax.nn.silu(c)

    for l in range(DEPTH):
        lam_init = 0.8 - 0.6 * math.exp(-0.3 * l)
        mod = cond @ w_ada[l] + b_ada[l]
        sh1, sc1, g1, sh2, sc2, g2, sh3, sc3, g3 = jnp.split(mod, N_MOD, axis=-1)

        h = modulate(rms_norm(x, ffn1_norm[l]), sh1, sc1)
        x = x + 0.5 * g1[:, None, :] * swiglu(h, ffn1_w1[l], ffn1_w3[l], ffn1_w2[l])

        h = modulate(rms_norm(x, mix_norm[l]), sh2, sc2)
        proj = h @ w_in[l]
        q = proj[..., :ATTN_WIDTH].reshape(B, L, N_HEADS, 2, HEAD_DIM)
        k = proj[..., ATTN_WIDTH:2 * ATTN_WIDTH].reshape(B, L, N_HEADS, 2, HEAD_DIM)
        v = proj[..., 2 * ATTN_WIDTH:3 * ATTN_WIDTH].reshape(B, L, N_HEADS, V_DIM)
        u = proj[..., 3 * ATTN_WIDTH:]

        q = rope_partial(rms_norm(q, q_norm[l]), cos, sin)
        k = rope_partial(rms_norm(k, k_norm[l]), cos, sin)
        lam = (jnp.exp(jnp.sum(lambda_q1[l].astype(f32) * lambda_k1[l].astype(f32)))
               - jnp.exp(jnp.sum(lambda_q2[l].astype(f32) * lambda_k2[l].astype(f32)))
               + lam_init)
        attn_out = diff_attention(q, k, v, chunk_id, lam, lam_init, attn_subln[l])

        y = s5_ssm(u, ssm_a_re[l], ssm_a_im[l], ssm_log_dt[l], ssm_b_re[l], ssm_b_im[l],
                   ssm_c_re[l], ssm_c_im[l], ssm_d[l])
        y = jax.nn.gelu(y)
        y = y * jax.nn.sigmoid(y @ w_glu[l] + b_glu[l])
        y = rms_norm(y, ssm_out_norm[l])

        mixed = jnp.concatenate([attn_out, y], axis=-1) @ w_out[l]
        x = x + g2[:, None, :] * mixed

        h = modulate(rms_norm(x, ffn2_norm[l]), sh3, sc3)
        x = x + 0.5 * g3[:, None, :] * swiglu(h, ffn2_w1[l], ffn2_w3[l], ffn2_w2[l])
    return x
```

```python
import functools
import math

import jax
import jax.numpy as jnp
from jax import lax
from jax.experimental import pallas as pl
from jax.experimental.pallas import tpu as pltpu

F32 = jnp.float32
BF16 = jnp.bfloat16

CHUNK = 64
N_HEADS = 8
HEAD_DIM = 64
V_DIM = 128
ROT_DIM = 16
ROPE_THETA = 500000.0
SSM_GROUP = 16
SSM_STATE = 64
EPS = 1e-6
NEG_INF = -1e30
LOG2E = 1.4426950408889634

SSM_T = 16
SSM_W = SSM_T * SSM_GROUP
SCAN_LEVELS = 10

VMEM_LIMIT = 56 * 1024 * 1024


def _cparams(sem):
    return pltpu.CompilerParams(dimension_semantics=sem, vmem_limit_bytes=VMEM_LIMIT)


def _sigmoid(x):
    return 1.0 / (1.0 + jnp.exp(-x))


def _norm_mod(x, g, shift, scale):
    ms = jnp.mean(x * x, axis=-1, keepdims=True)
    y = x * lax.rsqrt(ms + EPS) * g
    return y * (1.0 + scale) + shift


def _ada_kernel(c_ref, w_ref, b_ref, o_ref):
    c = c_ref[...]
    cond = c * _sigmoid(c)
    o_ref[...] = jnp.dot(cond, w_ref[...], preferred_element_type=F32,
                         precision=lax.Precision.HIGHEST) + b_ref[...]


def _ada(c, w_ada, b_ada, tn=1024):
    d, n = w_ada.shape
    c8 = jnp.broadcast_to(c, (8, d))
    out = pl.pallas_call(
        _ada_kernel,
        out_shape=jax.ShapeDtypeStruct((8, n), F32),
        grid=(n // tn,),
        in_specs=[pl.BlockSpec((8, d), lambda j: (0, 0)),
                  pl.BlockSpec((d, tn), lambda j: (0, j)),
                  pl.BlockSpec((1, tn), lambda j: (0, j))],
        out_specs=pl.BlockSpec((8, tn), lambda j: (0, j)),
        compiler_params=_cparams(("arbitrary",)),
        name="ada",
    )(c8, w_ada, b_ada.reshape(1, n))
    return out[0:1]


def _ffn_kernel(x_ref, g_ref, sh_ref, sc_ref, gate_ref, w1_ref, w3_ref, w2_ref,
                o_ref, h_ref, acc_ref):
    f = pl.program_id(1)

    @pl.when(f == 0)
    def _():
        h = _norm_mod(x_ref[...], g_ref[...], sh_ref[...], sc_ref[...])
        h_ref[...] = h.astype(BF16)
        acc_ref[...] = jnp.zeros_like(acc_ref)

    h = h_ref[...]
    a = jnp.dot(h, w1_ref[...], preferred_element_type=F32)
    b = jnp.dot(h, w3_ref[...], preferred_element_type=F32)
    g = (a * _sigmoid(a)) * b
    acc_ref[...] += jnp.dot(g.astype(BF16), w2_ref[...], preferred_element_type=F32)

    @pl.when(f == pl.num_programs(1) - 1)
    def _():
        o_ref[...] = x_ref[...] + (0.5 * gate_ref[...]) * acc_ref[...]


def _ffn(x, g, shift, scale, gate, w1, w3, w2, tm=512, tf=512):
    l, d = x.shape
    dff = w1.shape[1]
    row = lambda i, f: (i, 0)
    vec = lambda i, f: (0, 0)
    return pl.pallas_call(
        _ffn_kernel,
        out_shape=jax.ShapeDtypeStruct((l, d), F32),
        grid=(l // tm, dff // tf),
        in_specs=[pl.BlockSpec((tm, d), row),
                  pl.BlockSpec((1, d), vec), pl.BlockSpec((1, d), vec),
                  pl.BlockSpec((1, d), vec), pl.BlockSpec((1, d), vec),
                  pl.BlockSpec((d, tf), lambda i, f: (0, f)),
                  pl.BlockSpec((d, tf), lambda i, f: (0, f)),
                  pl.BlockSpec((tf, d), lambda i, f: (f, 0))],
        out_specs=pl.BlockSpec((tm, d), row),
        scratch_shapes=[pltpu.VMEM((tm, d), BF16), pltpu.VMEM((tm, d), F32)],
        compiler_params=_cparams(("parallel", "arbitrary")),
        name="ffn",
    )(x, g, shift, scale, gate, w1, w3, w2)


def _proj_kernel(x_ref, g_ref, sh_ref, sc_ref, w_ref, qn_ref, kn_ref, cos_ref, s1_ref,
                 s2_ref, gsum_ref, q_ref, k_ref, v_ref, u_ref, h_ref, *, q_scale):
    j = pl.program_id(1)

    @pl.when(j == 0)
    def _():
        h = _norm_mod(x_ref[...], g_ref[...], sh_ref[...], sc_ref[...])
        h_ref[...] = h.astype(BF16)

    p = jnp.dot(h_ref[...], w_ref[...], preferred_element_type=F32)

    def qk_out(gain_ref, out_ref, scale):
        cos, s1, s2 = cos_ref[...], s1_ref[...], s2_ref[...]
        gain = gain_ref[...] * scale
        for hd in range(N_HEADS):
            pc = p[:, hd * V_DIM:(hd + 1) * V_DIM]
            ss = jnp.dot((pc * pc).astype(BF16), gsum_ref[...], preferred_element_type=F32)
            y = pc * lax.rsqrt(ss * (1.0 / HEAD_DIM) + EPS) * gain
            r = (y * cos + pltpu.roll(y, V_DIM - ROT_DIM // 2, 1) * s1
                 + pltpu.roll(y, ROT_DIM // 2, 1) * s2)
            out_ref[hd] = r.astype(BF16)

    @pl.when(j == 0)
    def _():
        qk_out(qn_ref, q_ref, q_scale)

    @pl.when(j == 1)
    def _():
        qk_out(kn_ref, k_ref, 1.0)

    @pl.when(j == 2)
    def _():
        for hd in range(N_HEADS):
            v_ref[hd] = p[:, hd * V_DIM:(hd + 1) * V_DIM].astype(BF16)

    @pl.when(j == 3)
    def _():
        u_ref[...] = p.astype(BF16)


def _proj(x, g, shift, scale, w_in, qn, kn, cos_t, s1_t, s2_t, gsum, tm=512):
    l, d = x.shape
    tn = 1024
    row = lambda i, j: (i, 0)
    vec = lambda i, j: (0, 0)
    hm = lambda i, j: (0, i, 0)
    head_major = jax.ShapeDtypeStruct((N_HEADS, l, V_DIM), BF16)
    return pl.pallas_call(
        functools.partial(_proj_kernel, q_scale=HEAD_DIM ** -0.5 * LOG2E),
        out_shape=(head_major, head_major, head_major, jax.ShapeDtypeStruct((l, tn), BF16)),
        grid=(l // tm, 4),
        in_specs=[pl.BlockSpec((tm, d), row),
                  pl.BlockSpec((1, d), vec), pl.BlockSpec((1, d), vec), pl.BlockSpec((1, d), vec),
                  pl.BlockSpec((d, tn), lambda i, j: (0, j)),
                  pl.BlockSpec((1, V_DIM), vec), pl.BlockSpec((1, V_DIM), vec),
                  pl.BlockSpec((tm, V_DIM), row), pl.BlockSpec((tm, V_DIM), row),
                  pl.BlockSpec((tm, V_DIM), row),
                  pl.BlockSpec((V_DIM, V_DIM), vec)],
        out_specs=(pl.BlockSpec((N_HEADS, tm, V_DIM), hm), pl.BlockSpec((N_HEADS, tm, V_DIM), hm),
                   pl.BlockSpec((N_HEADS, tm, V_DIM), hm), pl.BlockSpec((tm, tn), row)),
        scratch_shapes=[pltpu.VMEM((tm, d), BF16)],
        compiler_params=_cparams(("parallel", "arbitrary")),
        name="proj",
    )(x, g, shift, scale, w_in, qn, kn, cos_t, s1_t, s2_t, gsum)


def _attn_kernel(q_ref, k_ref, v_ref, cq_ref, ck_ref, lamv_ref, subg_ref, o_ref,
                 m_ref, l_ref, acc_ref, *, tq, lam_init):
    i = pl.program_id(1)
    q = q_ref[0]
    lane = lax.broadcasted_iota(jnp.int32, q.shape, 1)
    zero = jnp.zeros_like(q)
    q2 = jnp.concatenate([jnp.where(lane < HEAD_DIM, q, zero),
                          jnp.where(lane >= HEAD_DIM, q, zero)], axis=0)
    m_ref[...] = jnp.full(m_ref.shape, NEG_INF, F32)
    l_ref[...] = jnp.zeros_like(l_ref)
    acc_ref[...] = jnp.zeros_like(acc_ref)

    def step(j, masked):
        off = pl.multiple_of(j * tq, tq)
        kt = k_ref[0, pl.ds(off, tq), :]
        vt = v_ref[0, pl.ds(off, tq), :]
        s = lax.dot_general(q2, kt, (((1,), (1,)), ((), ())), preferred_element_type=F32)
        if masked:
            cq = cq_ref[...]
            cq2 = jnp.concatenate([cq, cq], axis=0)
            s = jnp.where(ck_ref[...] <= cq2, s, NEG_INF)
        m_prev = m_ref[...]
        m_new = jnp.maximum(m_prev, jnp.max(s, axis=1, keepdims=True))
        alpha = jnp.exp2(m_prev - m_new)
        p = jnp.exp2(s - m_new)
        l_ref[...] = alpha * l_ref[...] + jnp.sum(p, axis=1, keepdims=True)
        acc_ref[...] = alpha * acc_ref[...] + jnp.dot(p.astype(BF16), vt,
                                                      preferred_element_type=F32)
        m_ref[...] = m_new

    def body(j, carry):
        step(j, False)
        return carry

    lax.fori_loop(0, i, body, 0)
    step(i, True)

    o = acc_ref[...] / l_ref[...]
    lv = lamv_ref[...]
    lam = (jnp.exp(jnp.sum(lv[0:1] * lv[1:2], axis=1, keepdims=True))
           - jnp.exp(jnp.sum(lv[2:3] * lv[3:4], axis=1, keepdims=True)) + lam_init)
    od = o[:tq] - lam * o[tq:]
    ms = jnp.mean(od * od, axis=-1, keepdims=True)
    o_ref[...] = (od * lax.rsqrt(ms + EPS) * subg_ref[...] * (1.0 - lam_init)).astype(BF16)


def _attention(q, k, v, cid_col, cid_row, lamv, subg, lam_init, tq=512):
    _, l, _ = q.shape
    return pl.pallas_call(
        functools.partial(_attn_kernel, tq=tq, lam_init=lam_init),
        out_shape=jax.ShapeDtypeStruct((l, N_HEADS * V_DIM), BF16),
        grid=(N_HEADS, l // tq),
        in_specs=[pl.BlockSpec((1, tq, V_DIM), lambda h, i: (h, i, 0)),
                  pl.BlockSpec((1, l, V_DIM), lambda h, i: (h, 0, 0)),
                  pl.BlockSpec((1, l, V_DIM), lambda h, i: (h, 0, 0)),
                  pl.BlockSpec((tq, 1), lambda h, i: (i, 0)),
                  pl.BlockSpec((1, tq), lambda h, i: (0, i)),
                  pl.BlockSpec((4, HEAD_DIM), lambda h, i: (0, 0)),
                  pl.BlockSpec((1, V_DIM), lambda h, i: (0, 0))],
        out_specs=pl.BlockSpec((tq, V_DIM), lambda h, i: (i, h)),
        scratch_shapes=[pltpu.VMEM((2 * tq, 1), F32), pltpu.VMEM((2 * tq, 1), F32),
                        pltpu.VMEM((2 * tq, V_DIM), F32)],
        compiler_params=_cparams(("parallel", "arbitrary")),
        name="attn",
    )(q, k, v, cid_col, cid_row, lamv, subg)


def _ssm_kernel(u_ref, at_ref, bm_ref, cm_ref, pr_ref, pi_ref, y_ref):
    u = u_ref[0]
    nb = u.shape[0]
    x = jnp.dot(u, bm_ref[0], preferred_element_type=F32)
    rows = lax.broadcasted_iota(jnp.int32, x.shape, 0)

    def shift_down(z, d):
        if d % 8 == 0:
            return jnp.concatenate([jnp.zeros((d, z.shape[1]), F32), z[:nb - d]], axis=0)
        return jnp.where(rows >= d, pltpu.roll(z, d, 0), 0.0)

    pr, pi = pr_ref[0], pi_ref[0]
    for lvl in range(SCAN_LEVELS):
        z = shift_down(x, 1 << lvl)
        x = x + pr[lvl:lvl + 1] * z + pi[lvl:lvl + 1] * pltpu.roll(z, SSM_STATE, 1)
    xprev = shift_down(x, 1)
    y = jnp.dot(u, at_ref[0], preferred_element_type=F32)
    y = y + jnp.dot(xprev.astype(BF16), cm_ref[0], preferred_element_type=F32)
    y_ref[0] = y


def _ssm(u_blk, at, bm, cm, pr, pi):
    g, nb, w = u_blk.shape
    blk = lambda a: pl.BlockSpec((1,) + a.shape[1:], lambda i: (i, 0, 0))
    return pl.pallas_call(
        _ssm_kernel,
        out_shape=jax.ShapeDtypeStruct((g, nb, w), F32),
        grid=(g,),
        in_specs=[blk(u_blk), blk(at), blk(bm), blk(cm), blk(pr), blk(pi)],
        out_specs=pl.BlockSpec((1, nb, w), lambda i: (i, 0, 0)),
        compiler_params=_cparams(("parallel",)),
        name="ssm",
    )(u_blk, at, bm, cm, pr, pi)


def _ssm_prep(a_re, a_im, log_dt, b_re, b_im, c_re, c_im, d_skip):
    hp = lax.Precision.HIGHEST
    lr = jnp.minimum(a_re, -1e-4)
    li = a_im
    dt = jnp.exp(log_dt)[:, None]

    def lpow(n):
        n = jnp.asarray(n, F32)[..., None, None]
        mag = jnp.exp(n * (lr * dt))
        ang = n * (li * dt)
        return mag * jnp.cos(ang), mag * jnp.sin(ang)

    th = li * dt
    nr = jnp.expm1(lr * dt) * jnp.cos(th) - 2.0 * jnp.sin(0.5 * th) ** 2
    ni = jnp.exp(lr * dt) * jnp.sin(th)
    den = lr * lr + li * li
    fr = (nr * lr + ni * li) / den
    fi = (ni * lr - nr * li) / den
    bbr = fr[..., None] * b_re - fi[..., None] * b_im
    bbi = fr[..., None] * b_im + fi[..., None] * b_re

    t = jnp.arange(SSM_T)
    pjr, pji = lpow(t)
    mr = pjr[..., None] * bbr[None] - pji[..., None] * bbi[None]
    mi = pjr[..., None] * bbi[None] + pji[..., None] * bbr[None]
    kj = (jnp.einsum('gcp,jgpd->gjcd', c_re, mr, precision=hp)
          - jnp.einsum('gcp,jgpd->gjcd', c_im, mi, precision=hp))
    lag = t[None, :] - t[:, None]
    kt = kj[:, jnp.clip(lag, 0, SSM_T - 1)]
    kt = jnp.where((lag >= 0)[None, :, :, None, None], kt, 0.0)
    eye_t = jnp.eye(SSM_T, dtype=F32)
    eye_c = jnp.eye(SSM_GROUP, dtype=F32)
    kt = kt + (eye_t[None, :, :, None, None] * eye_c[None, None, None]
               * d_skip[:, None, None, :, None])
    g = a_re.shape[0]
    at = kt.transpose(0, 1, 4, 2, 3).reshape(g, SSM_W, SSM_W)

    rr, ri = lpow(SSM_T - 1 - t)
    sr = rr[..., None] * bbr[None] - ri[..., None] * bbi[None]
    si = rr[..., None] * bbi[None] + ri[..., None] * bbr[None]
    bm = jnp.concatenate([sr, si], axis=2)
    bm = bm.transpose(1, 0, 3, 2).reshape(g, SSM_W, 2 * SSM_STATE)

    qr, qi = lpow(t + 1)
    wr = c_re[None] * qr[:, :, None, :] - c_im[None] * qi[:, :, None, :]
    wi = c_re[None] * qi[:, :, None, :] + c_im[None] * qr[:, :, None, :]
    cm = jnp.concatenate([wr, -wi], axis=3)
    cm = cm.transpose(1, 3, 0, 2).reshape(g, 2 * SSM_STATE, SSM_W)

    er, ei = lpow(SSM_T * (2 ** jnp.arange(16)))
    pr = jnp.concatenate([er, er], axis=2).transpose(1, 0, 2)
    pi = jnp.concatenate([-ei, ei], axis=2).transpose(1, 0, 2)
    return at.astype(BF16), bm.astype(BF16), cm.astype(BF16), pr, pi


def _post_kernel(x_ref, attn_ref, y_ref, wglu_ref, bglu_ref, sn_ref, woa_ref, wob_ref,
                 g2_ref, o_ref):
    y = jax.nn.gelu(y_ref[...], approximate=True)
    z = jnp.dot(y.astype(BF16), wglu_ref[...], preferred_element_type=F32) + bglu_ref[...]
    y = y * _sigmoid(z)
    ms = jnp.mean(y * y, axis=-1, keepdims=True)
    y = y * lax.rsqrt(ms + EPS) * sn_ref[...]
    mixed = (jnp.dot(attn_ref[...], woa_ref[...], preferred_element_type=F32)
             + jnp.dot(y.astype(BF16), wob_ref[...], preferred_element_type=F32))
    o_ref[...] = x_ref[...] + g2_ref[...] * mixed


def _post(x, attn, y, w_glu, b_glu, sn, w_out, g2, tm=256):
    l, d = x.shape
    w = y.shape[1]
    row = lambda i: (i, 0)
    vec = lambda i: (0, 0)
    return pl.pallas_call(
        _post_kernel,
        out_shape=jax.ShapeDtypeStruct((l, d), F32),
        grid=(l // tm,),
        in_specs=[pl.BlockSpec((tm, d), row), pl.BlockSpec((tm, w), row), pl.BlockSpec((tm, w), row),
                  pl.BlockSpec((w, w), vec), pl.BlockSpec((1, w), vec), pl.BlockSpec((1, w), vec),
                  pl.BlockSpec((w, d), lambda i: (0, 0)), pl.BlockSpec((w, d), lambda i: (1, 0)),
                  pl.BlockSpec((1, d), vec)],
        out_specs=pl.BlockSpec((tm, d), row),
        compiler_params=_cparams(("parallel",)),
        name="post",
    )(x, attn, y, w_glu, b_glu, sn, w_out, w_out, g2)


def _rope_tables(positions):
    inv_freq = ROPE_THETA ** (-jnp.arange(0, ROT_DIM, 2, dtype=F32) / ROT_DIM)
    ang = positions.astype(F32)[:, None] * inv_freq
    cos, sin = jnp.cos(ang), jnp.sin(ang)
    l = positions.shape[0]
    half = ROT_DIM // 2
    ones = jnp.ones((l, HEAD_DIM - ROT_DIM), F32)
    zeros = jnp.zeros((l, HEAD_DIM - ROT_DIM), F32)
    zh = jnp.zeros((l, half), F32)
    cos_t = jnp.concatenate([cos, cos, ones], axis=1)
    s1_t = jnp.concatenate([-sin, zh, zeros], axis=1)
    s2_t = jnp.concatenate([zh, sin, zeros], axis=1)
    tile2 = lambda a: jnp.concatenate([a, a], axis=1)
    return tile2(cos_t), tile2(s1_t), tile2(s2_t)


def kernel(x, c, positions, w_ada, b_ada, ffn1_norm, ffn1_w1, ffn1_w3, ffn1_w2, mix_norm, w_in, q_norm, k_norm, lambda_q1, lambda_k1, lambda_q2, lambda_k2, attn_subln, ssm_a_re, ssm_a_im, ssm_log_dt, ssm_b_re, ssm_b_im, ssm_c_re, ssm_c_im, ssm_d, w_glu, b_glu, ssm_out_norm, w_out, ffn2_norm, ffn2_w1, ffn2_w3, ffn2_w2):
    batch, seq, d = x.shape
    depth = w_ada.shape[0]
    assert batch == 1 and seq == SSM_T * 2 ** SCAN_LEVELS
    pos = positions[0]
    cid = pos // CHUNK
    cid_col, cid_row = cid.reshape(seq, 1), cid.reshape(1, seq)
    cos_t, s1_t, s2_t = _rope_tables(pos)
    lane = jnp.arange(V_DIM)
    gsum = (lane[:, None] // HEAD_DIM == lane[None, :] // HEAD_DIM).astype(BF16)
    tile2 = lambda a: jnp.concatenate([a, a]).reshape(1, V_DIM)
    nb = seq // SSM_T
    n_groups = ssm_a_re.shape[1]

    xs = x[0]
    for l in range(depth):
        lam_init = 0.8 - 0.6 * math.exp(-0.3 * l)
        mod = _ada(c, w_ada[l], b_ada[l])
        sh1, sc1, g1, sh2, sc2, g2, sh3, sc3, g3 = jnp.split(mod, 9, axis=-1)
        vec = lambda a: a.reshape(1, -1)

        xs = _ffn(xs, vec(ffn1_norm[l]), sh1, sc1, g1, ffn1_w1[l].astype(BF16),
                  ffn1_w3[l].astype(BF16), ffn1_w2[l].astype(BF16))

        q, k, v, u = _proj(xs, vec(mix_norm[l]), sh2, sc2, w_in[l].astype(BF16),
                           tile2(q_norm[l]), tile2(k_norm[l]), cos_t, s1_t, s2_t, gsum)

        lamv = jnp.stack([lambda_q1[l], lambda_k1[l], lambda_q2[l], lambda_k2[l]]).astype(F32)
        attn = _attention(q, k, v, cid_col, cid_row, lamv, vec(attn_subln[l]), lam_init)

        at, bm, cm, pr, pi = _ssm_prep(ssm_a_re[l], ssm_a_im[l], ssm_log_dt[l], ssm_b_re[l],
                                       ssm_b_im[l], ssm_c_re[l], ssm_c_im[l], ssm_d[l])
        u_blk = (u.reshape(nb, SSM_T, n_groups, SSM_GROUP).transpose(2, 0, 1, 3)
                 .reshape(n_groups, nb, SSM_W))
        y_blk = _ssm(u_blk, at, bm, cm, pr, pi)
        y = (y_blk.reshape(n_groups, nb, SSM_T, SSM_GROUP).transpose(1, 2, 0, 3)
             .reshape(seq, n_groups * SSM_GROUP))

        xs = _post(xs, attn, y, w_glu[l].astype(BF16), vec(b_glu[l]), vec(ssm_out_norm[l]),
                   w_out[l].astype(BF16), g2)

        xs = _ffn(xs, vec(ffn2_norm[l]), sh3, sc3, g3, ffn2_w1[l].astype(BF16),
                  ffn2_w3[l].astype(BF16), ffn2_w2[l].astype(BF16))
    return xs[None]
```

```python
import functools
import math

import jax
import jax.numpy as jnp
from jax import lax
from jax.experimental import pallas as pl
from jax.experimental.pallas import tpu as pltpu

F32 = jnp.float32
BF16 = jnp.bfloat16

CHUNK = 64
N_HEADS = 8
HEAD_DIM = 64
V_DIM = 128
ROT_DIM = 16
ROPE_THETA = 500000.0
SSM_GROUP = 16
SSM_STATE = 64
EPS = 1e-6
NEG_INF = -1e30
LOG2E = 1.4426950408889634

SSM_T = 16
SSM_W = SSM_T * SSM_GROUP
SCAN_LEVELS = 10

VMEM_LIMIT = 56 * 1024 * 1024


def _cparams(sem):
    return pltpu.CompilerParams(dimension_semantics=sem, vmem_limit_bytes=VMEM_LIMIT)


def _sigmoid(x):
    return 1.0 / (1.0 + jnp.exp(-x))


def _norm_mod(x, g, shift, scale):
    ms = jnp.mean(x * x, axis=-1, keepdims=True)
    y = x * lax.rsqrt(ms + EPS) * g
    return y * (1.0 + scale) + shift


def _ada_kernel(c_ref, w_ref, b_ref, o_ref):
    c = c_ref[...]
    cond = c * _sigmoid(c)
    o_ref[...] = jnp.dot(cond, w_ref[...], preferred_element_type=F32,
                         precision=lax.Precision.HIGHEST) + b_ref[...]


def _ada(c, w_ada, b_ada, tn=1024):
    d, n = w_ada.shape
    c8 = jnp.broadcast_to(c, (8, d))
    out = pl.pallas_call(
        _ada_kernel,
        out_shape=jax.ShapeDtypeStruct((8, n), F32),
        grid=(n // tn,),
        in_specs=[pl.BlockSpec((8, d), lambda j: (0, 0)),
                  pl.BlockSpec((d, tn), lambda j: (0, j)),
                  pl.BlockSpec((1, tn), lambda j: (0, j))],
        out_specs=pl.BlockSpec((8, tn), lambda j: (0, j)),
        compiler_params=_cparams(("arbitrary",)),
        name="ada",
    )(c8, w_ada, b_ada.reshape(1, n))
    return out[0:1]


def _ffn_kernel(x_ref, g_ref, sh_ref, sc_ref, gate_ref, w1_ref, w3_ref, w2_ref,
                o_ref, h_ref, acc_ref):
    f = pl.program_id(1)

    @pl.when(f == 0)
    def _():
        h = _norm_mod(x_ref[...], g_ref[...], sh_ref[...], sc_ref[...])
        h_ref[...] = h.astype(BF16)
        acc_ref[...] = jnp.zeros_like(acc_ref)

    h = h_ref[...]
    a = jnp.dot(h, w1_ref[...], preferred_element_type=F32)
    b = jnp.dot(h, w3_ref[...], preferred_element_type=F32)
    g = (a * _sigmoid(a)) * b
    acc_ref[...] += jnp.dot(g.astype(BF16), w2_ref[...], preferred_element_type=F32)

    @pl.when(f == pl.num_programs(1) - 1)
    def _():
        o_ref[...] = x_ref[...] + (0.5 * gate_ref[...]) * acc_ref[...]


def _ffn(x, g, shift, scale, gate, w1, w3, w2, tm=512, tf=512):
    l, d = x.shape
    dff = w1.shape[1]
    row = lambda i, f: (i, 0)
    vec = lambda i, f: (0, 0)
    return pl.pallas_call(
        _ffn_kernel,
        out_shape=jax.ShapeDtypeStruct((l, d), F32),
        grid=(l // tm, dff // tf),
        in_specs=[pl.BlockSpec((tm, d), row),
                  pl.BlockSpec((1, d), vec), pl.BlockSpec((1, d), vec),
                  pl.BlockSpec((1, d), vec), pl.BlockSpec((1, d), vec),
                  pl.BlockSpec((d, tf), lambda i, f: (0, f)),
                  pl.BlockSpec((d, tf), lambda i, f: (0, f)),
                  pl.BlockSpec((tf, d), lambda i, f: (f, 0))],
        out_specs=pl.BlockSpec((tm, d), row),
        scratch_shapes=[pltpu.VMEM((tm, d), BF16), pltpu.VMEM((tm, d), F32)],
        compiler_params=_cparams(("parallel", "arbitrary")),
        name="ffn",
    )(x, g, shift, scale, gate, w1, w3, w2)


def _proj_kernel(x_ref, g_ref, sh_ref, sc_ref, w_ref, qn_ref, kn_ref, cos_ref, s1_ref,
                 s2_ref, gsum_ref, qt_ref, k_ref, vt_ref, u_ref, h_ref, *, q_scale):
    j = pl.program_id(1)

    @pl.when(j == 0)
    def _():
        h = _norm_mod(x_ref[...], g_ref[...], sh_ref[...], sc_ref[...])
        h_ref[...] = h.astype(BF16)

    p = jnp.dot(h_ref[...], w_ref[...], preferred_element_type=F32)

    def qk_rot(gain_ref, scale, hd):
        pc = p[:, hd * V_DIM:(hd + 1) * V_DIM]
        ss = jnp.dot((pc * pc).astype(BF16), gsum_ref[...], preferred_element_type=F32)
        y = pc * lax.rsqrt(ss * (1.0 / HEAD_DIM) + EPS) * (gain_ref[...] * scale)
        return (y * cos_ref[...] + pltpu.roll(y, V_DIM - ROT_DIM // 2, 1) * s1_ref[...]
                + pltpu.roll(y, ROT_DIM // 2, 1) * s2_ref[...])

    @pl.when(j == 0)
    def _():
        for hd in range(N_HEADS):
            qt_ref[hd, 0] = qk_rot(qn_ref, q_scale, hd).T.astype(BF16)

    @pl.when(j == 1)
    def _():
        for hd in range(N_HEADS):
            k_ref[hd] = qk_rot(kn_ref, 1.0, hd).astype(BF16)

    @pl.when(j == 2)
    def _():
        for hd in range(N_HEADS):
            vt_ref[hd, 0] = p[:, hd * V_DIM:(hd + 1) * V_DIM].T.astype(BF16)

    @pl.when(j == 3)
    def _():
        u_ref[...] = p.astype(BF16)


def _proj(x, g, shift, scale, w_in, qn, kn, cos_t, s1_t, s2_t, gsum, tm=512):
    l, d = x.shape
    tn = 1024
    row = lambda i, j: (i, 0)
    vec = lambda i, j: (0, 0)
    hm = lambda i, j: (0, i, 0)
    tm_t = lambda i, j: (0, i, 0, 0)
    head_major = jax.ShapeDtypeStruct((N_HEADS, l, V_DIM), BF16)
    tiled_t = jax.ShapeDtypeStruct((N_HEADS, l // tm, V_DIM, tm), BF16)
    return pl.pallas_call(
        functools.partial(_proj_kernel, q_scale=HEAD_DIM ** -0.5 * LOG2E),
        out_shape=(tiled_t, head_major, tiled_t, jax.ShapeDtypeStruct((l, tn), BF16)),
        grid=(l // tm, 4),
        in_specs=[pl.BlockSpec((tm, d), row),
                  pl.BlockSpec((1, d), vec), pl.BlockSpec((1, d), vec), pl.BlockSpec((1, d), vec),
                  pl.BlockSpec((d, tn), lambda i, j: (0, j)),
                  pl.BlockSpec((1, V_DIM), vec), pl.BlockSpec((1, V_DIM), vec),
                  pl.BlockSpec((tm, V_DIM), row), pl.BlockSpec((tm, V_DIM), row),
                  pl.BlockSpec((tm, V_DIM), row),
                  pl.BlockSpec((V_DIM, V_DIM), vec)],
        out_specs=(pl.BlockSpec((N_HEADS, 1, V_DIM, tm), tm_t), pl.BlockSpec((N_HEADS, tm, V_DIM), hm),
                   pl.BlockSpec((N_HEADS, 1, V_DIM, tm), tm_t), pl.BlockSpec((tm, tn), row)),
        scratch_shapes=[pltpu.VMEM((tm, d), BF16)],
        compiler_params=_cparams(("parallel", "arbitrary")),
        name="proj",
    )(x, g, shift, scale, w_in, qn, kn, cos_t, s1_t, s2_t, gsum)


ATTN_CW = 256


def _tree8(x, op):
    while x.shape[0] > 8:
        h = x.shape[0] // 2
        x = op(x[:h], x[h:])
    return x


def _attn_kernel(qt_ref, k_ref, vt_ref, cq_ref, ck_ref, lamv_ref, subg_ref, o_ref,
                 q2_ref, m_ref, l_ref, acc_ref, *, tq, lam_init):
    i = pl.program_id(1)
    qt = qt_ref[0, 0]
    feat = lax.broadcasted_iota(jnp.int32, qt.shape, 0)
    zero = jnp.zeros_like(qt)
    q2_ref[:, :tq] = jnp.where(feat < HEAD_DIM, qt, zero)
    q2_ref[:, tq:] = jnp.where(feat >= HEAD_DIM, qt, zero)
    m_ref[...] = jnp.full(m_ref.shape, NEG_INF, F32)
    l_ref[...] = jnp.zeros_like(l_ref)
    acc_ref[...] = jnp.zeros_like(acc_ref)

    n_sub = 2 * tq // ATTN_CW

    def step(j, masked):
        kt = k_ref[0, pl.ds(pl.multiple_of(j * tq, tq), tq), :]
        vt = vt_ref[0, j]

        def scores(c):
            s = jnp.dot(kt, q2_ref[:, c * ATTN_CW:(c + 1) * ATTN_CW],
                        preferred_element_type=F32)
            if masked:
                q0 = (c * ATTN_CW) % tq
                s = jnp.where(ck_ref[...] <= cq_ref[:, q0:q0 + ATTN_CW], s, NEG_INF)
            return s

        s_next = scores(0)
        for c in range(n_sub):
            cs = slice(c * ATTN_CW, (c + 1) * ATTN_CW)
            s = s_next
            if c + 1 < n_sub:
                s_next = scores(c + 1)
            m_prev = m_ref[:, cs]
            m_new = jnp.maximum(m_prev, jnp.max(_tree8(s, jnp.maximum), axis=0, keepdims=True))
            alpha = jnp.exp2(m_prev - m_new)
            p = jnp.exp2(s - m_new)
            l_ref[:, cs] = alpha * l_ref[:, cs] + jnp.sum(_tree8(p, jnp.add), axis=0,
                                                          keepdims=True)
            acc_ref[:, cs] = alpha * acc_ref[:, cs] + jnp.dot(vt, p.astype(BF16),
                                                              preferred_element_type=F32)
            m_ref[:, cs] = m_new

    def body(j, carry):
        step(j, False)
        return carry

    lax.fori_loop(0, i, body, 0)
    step(i, True)

    o = acc_ref[...] / l_ref[...]
    lv = lamv_ref[...]
    lam = (jnp.exp(jnp.sum(lv[0:1] * lv[1:2], axis=1, keepdims=True))
           - jnp.exp(jnp.sum(lv[2:3] * lv[3:4], axis=1, keepdims=True)) + lam_init)
    od = o[:, :tq] - lam * o[:, tq:]
    ms = jnp.mean(od * od, axis=0, keepdims=True)
    on = od * lax.rsqrt(ms + EPS) * (subg_ref[...] * (1.0 - lam_init))
    o_ref[...] = on.T.astype(BF16)


def _attention(qt, k, vt, cid_col, cid_row, lamv, subg_col, lam_init):
    _, nt, _, tq = qt.shape
    l = nt * tq
    return pl.pallas_call(
        functools.partial(_attn_kernel, tq=tq, lam_init=lam_init),
        out_shape=jax.ShapeDtypeStruct((l, N_HEADS * V_DIM), BF16),
        grid=(N_HEADS, nt),
        in_specs=[pl.BlockSpec((1, 1, V_DIM, tq), lambda h, i: (h, i, 0, 0)),
                  pl.BlockSpec((1, l, V_DIM), lambda h, i: (h, 0, 0)),
                  pl.BlockSpec((1, nt, V_DIM, tq), lambda h, i: (h, 0, 0, 0)),
                  pl.BlockSpec((1, tq), lambda h, i: (0, i)),
                  pl.BlockSpec((tq, 1), lambda h, i: (i, 0)),
                  pl.BlockSpec((4, HEAD_DIM), lambda h, i: (0, 0)),
                  pl.BlockSpec((V_DIM, 1), lambda h, i: (0, 0))],
        out_specs=pl.BlockSpec((tq, V_DIM), lambda h, i: (i, h)),
        scratch_shapes=[pltpu.VMEM((V_DIM, 2 * tq), BF16), pltpu.VMEM((1, 2 * tq), F32),
                        pltpu.VMEM((1, 2 * tq), F32), pltpu.VMEM((V_DIM, 2 * tq), F32)],
        compiler_params=_cparams(("parallel", "arbitrary")),
        name="attn",
    )(qt, k, vt, cid_row, cid_col, lamv, subg_col)


def _ssm_kernel(u_ref, at_ref, bm_ref, cm_ref, pr_ref, pi_ref, y_ref):
    u = u_ref[0]
    nb = u.shape[0]
    x = jnp.dot(u, bm_ref[0], preferred_element_type=F32)
    rows = lax.broadcasted_iota(jnp.int32, x.shape, 0)

    def shift_down(z, d):
        if d % 8 == 0:
            return jnp.concatenate([jnp.zeros((d, z.shape[1]), F32), z[:nb - d]], axis=0)
        return jnp.where(rows >= d, pltpu.roll(z, d, 0), 0.0)

    pr, pi = pr_ref[0], pi_ref[0]
    for lvl in range(SCAN_LEVELS):
        z = shift_down(x, 1 << lvl)
        x = x + pr[lvl:lvl + 1] * z + pi[lvl:lvl + 1] * pltpu.roll(z, SSM_STATE, 1)
    xprev = shift_down(x, 1)
    y = jnp.dot(u, at_ref[0], preferred_element_type=F32)
    y = y + jnp.dot(xprev.astype(BF16), cm_ref[0], preferred_element_type=F32)
    y_ref[0] = y


def _ssm(u_blk, at, bm, cm, pr, pi):
    g, nb, w = u_blk.shape
    blk = lambda a: pl.BlockSpec((1,) + a.shape[1:], lambda i: (i, 0, 0))
    return pl.pallas_call(
        _ssm_kernel,
        out_shape=jax.ShapeDtypeStruct((g, nb, w), F32),
        grid=(g,),
        in_specs=[blk(u_blk), blk(at), blk(bm), blk(cm), blk(pr), blk(pi)],
        out_specs=pl.BlockSpec((1, nb, w), lambda i: (i, 0, 0)),
        compiler_params=_cparams(("parallel",)),
        name="ssm",
    )(u_blk, at, bm, cm, pr, pi)


def _ssm_prep(a_re, a_im, log_dt, b_re, b_im, c_re, c_im, d_skip):
    hp = lax.Precision.HIGHEST
    lr = jnp.minimum(a_re, -1e-4)
    li = a_im
    dt = jnp.exp(log_dt)[:, None]

    def lpow(n):
        n = jnp.asarray(n, F32)[..., None, None]
        mag = jnp.exp(n * (lr * dt))
        ang = n * (li * dt)
        return mag * jnp.cos(ang), mag * jnp.sin(ang)

    th = li * dt
    nr = jnp.expm1(lr * dt) * jnp.cos(th) - 2.0 * jnp.sin(0.5 * th) ** 2
    ni = jnp.exp(lr * dt) * jnp.sin(th)
    den = lr * lr + li * li
    fr = (nr * lr + ni * li) / den
    fi = (ni * lr - nr * li) / den
    bbr = fr[..., None] * b_re - fi[..., None] * b_im
    bbi = fr[..., None] * b_im + fi[..., None] * b_re

    t = jnp.arange(SSM_T)
    pjr, pji = lpow(t)
    mr = pjr[..., None] * bbr[None] - pji[..., None] * bbi[None]
    mi = pjr[..., None] * bbi[None] + pji[..., None] * bbr[None]
    kj = (jnp.einsum('gcp,jgpd->gjcd', c_re, mr, precision=hp)
          - jnp.einsum('gcp,jgpd->gjcd', c_im, mi, precision=hp))
    lag = t[None, :] - t[:, None]
    kt = kj[:, jnp.clip(lag, 0, SSM_T - 1)]
    kt = jnp.where((lag >= 0)[None, :, :, None, None], kt, 0.0)
    eye_t = jnp.eye(SSM_T, dtype=F32)
    eye_c = jnp.eye(SSM_GROUP, dtype=F32)
    kt = kt + (eye_t[None, :, :, None, None] * eye_c[None, None, None]
               * d_skip[:, None, None, :, None])
    g = a_re.shape[0]
    at = kt.transpose(0, 1, 4, 2, 3).reshape(g, SSM_W, SSM_W)

    rr, ri = lpow(SSM_T - 1 - t)
    sr = rr[..., None] * bbr[None] - ri[..., None] * bbi[None]
    si = rr[..., None] * bbi[None] + ri[..., None] * bbr[None]
    bm = jnp.concatenate([sr, si], axis=2)
    bm = bm.transpose(1, 0, 3, 2).reshape(g, SSM_W, 2 * SSM_STATE)

    qr, qi = lpow(t + 1)
    wr = c_re[None] * qr[:, :, None, :] - c_im[None] * qi[:, :, None, :]
    wi = c_re[None] * qi[:, :, None, :] + c_im[None] * qr[:, :, None, :]
    cm = jnp.concatenate([wr, -wi], axis=3)
    cm = cm.transpose(1, 3, 0, 2).reshape(g, 2 * SSM_STATE, SSM_W)

    er, ei = lpow(SSM_T * (2 ** jnp.arange(16)))
    pr = jnp.concatenate([er, er], axis=2).transpose(1, 0, 2)
    pi = jnp.concatenate([-ei, ei], axis=2).transpose(1, 0, 2)
    return at.astype(BF16), bm.astype(BF16), cm.astype(BF16), pr, pi


def _post_kernel(x_ref, attn_ref, y_ref, wglu_ref, bglu_ref, sn_ref, woa_ref, wob_ref,
                 g2_ref, o_ref):
    y = jax.nn.gelu(y_ref[...], approximate=True)
    z = jnp.dot(y.astype(BF16), wglu_ref[...], preferred_element_type=F32) + bglu_ref[...]
    y = y * _sigmoid(z)
    ms = jnp.mean(y * y, axis=-1, keepdims=True)
    y = y * lax.rsqrt(ms + EPS) * sn_ref[...]
    mixed = (jnp.dot(attn_ref[...], woa_ref[...], preferred_element_type=F32)
             + jnp.dot(y.astype(BF16), wob_ref[...], preferred_element_type=F32))
    o_ref[...] = x_ref[...] + g2_ref[...] * mixed


def _post(x, attn, y, w_glu, b_glu, sn, w_out, g2, tm=256):
    l, d = x.shape
    w = y.shape[1]
    row = lambda i: (i, 0)
    vec = lambda i: (0, 0)
    return pl.pallas_call(
        _post_kernel,
        out_shape=jax.ShapeDtypeStruct((l, d), F32),
        grid=(l // tm,),
        in_specs=[pl.BlockSpec((tm, d), row), pl.BlockSpec((tm, w), row), pl.BlockSpec((tm, w), row),
                  pl.BlockSpec((w, w), vec), pl.BlockSpec((1, w), vec), pl.BlockSpec((1, w), vec),
                  pl.BlockSpec((w, d), lambda i: (0, 0)), pl.BlockSpec((w, d), lambda i: (1, 0)),
                  pl.BlockSpec((1, d), vec)],
        out_specs=pl.BlockSpec((tm, d), row),
        compiler_params=_cparams(("parallel",)),
        name="post",
    )(x, attn, y, w_glu, b_glu, sn, w_out, w_out, g2)


def _rope_tables(positions):
    inv_freq = ROPE_THETA ** (-jnp.arange(0, ROT_DIM, 2, dtype=F32) / ROT_DIM)
    ang = positions.astype(F32)[:, None] * inv_freq
    cos, sin = jnp.cos(ang), jnp.sin(ang)
    l = positions.shape[0]
    half = ROT_DIM // 2
    ones = jnp.ones((l, HEAD_DIM - ROT_DIM), F32)
    zeros = jnp.zeros((l, HEAD_DIM - ROT_DIM), F32)
    zh = jnp.zeros((l, half), F32)
    cos_t = jnp.concatenate([cos, cos, ones], axis=1)
    s1_t = jnp.concatenate([-sin, zh, zeros], axis=1)
    s2_t = jnp.concatenate([zh, sin, zeros], axis=1)
    tile2 = lambda a: jnp.concatenate([a, a], axis=1)
    return tile2(cos_t), tile2(s1_t), tile2(s2_t)


def kernel(x, c, positions, w_ada, b_ada, ffn1_norm, ffn1_w1, ffn1_w3, ffn1_w2, mix_norm, w_in, q_norm, k_norm, lambda_q1, lambda_k1, lambda_q2, lambda_k2, attn_subln, ssm_a_re, ssm_a_im, ssm_log_dt, ssm_b_re, ssm_b_im, ssm_c_re, ssm_c_im, ssm_d, w_glu, b_glu, ssm_out_norm, w_out, ffn2_norm, ffn2_w1, ffn2_w3, ffn2_w2):
    batch, seq, d = x.shape
    depth = w_ada.shape[0]
    assert batch == 1 and seq == SSM_T * 2 ** SCAN_LEVELS
    pos = positions[0]
    cid = pos // CHUNK
    cid_col, cid_row = cid.reshape(seq, 1), cid.reshape(1, seq)
    cos_t, s1_t, s2_t = _rope_tables(pos)
    lane = jnp.arange(V_DIM)
    gsum = (lane[:, None] // HEAD_DIM == lane[None, :] // HEAD_DIM).astype(BF16)
    tile2 = lambda a: jnp.concatenate([a, a]).reshape(1, V_DIM)
    nb = seq // SSM_T
    n_groups = ssm_a_re.shape[1]

    xs = x[0]
    for l in range(depth):
        lam_init = 0.8 - 0.6 * math.exp(-0.3 * l)
        mod = _ada(c, w_ada[l], b_ada[l])
        sh1, sc1, g1, sh2, sc2, g2, sh3, sc3, g3 = jnp.split(mod, 9, axis=-1)
        vec = lambda a: a.reshape(1, -1)

        xs = _ffn(xs, vec(ffn1_norm[l]), sh1, sc1, g1, ffn1_w1[l].astype(BF16),
                  ffn1_w3[l].astype(BF16), ffn1_w2[l].astype(BF16))

        qt, k, vt, u = _proj(xs, vec(mix_norm[l]), sh2, sc2, w_in[l].astype(BF16),
                             tile2(q_norm[l]), tile2(k_norm[l]), cos_t, s1_t, s2_t, gsum)

        lamv = jnp.stack([lambda_q1[l], lambda_k1[l], lambda_q2[l], lambda_k2[l]]).astype(F32)
        attn = _attention(qt, k, vt, cid_col, cid_row, lamv,
                          attn_subln[l].astype(F32).reshape(V_DIM, 1), lam_init)

        at, bm, cm, pr, pi = _ssm_prep(ssm_a_re[l], ssm_a_im[l], ssm_log_dt[l], ssm_b_re[l],
                                       ssm_b_im[l], ssm_c_re[l], ssm_c_im[l], ssm_d[l])
        u_blk = (u.reshape(nb, SSM_T, n_groups, SSM_GROUP).transpose(2, 0, 1, 3)
                 .reshape(n_groups, nb, SSM_W))
        y_blk = _ssm(u_blk, at, bm, cm, pr, pi)
        y = (y_blk.reshape(n_groups, nb, SSM_T, SSM_GROUP).transpose(1, 2, 0, 3)
             .reshape(seq, n_groups * SSM_GROUP))

        xs = _post(xs, attn, y, w_glu[l].astype(BF16), vec(b_glu[l]), vec(ssm_out_norm[l]),
                   w_out[l].astype(BF16), g2)

        xs = _ffn(xs, vec(ffn2_norm[l]), sh3, sc3, g3, ffn2_w1[l].astype(BF16),
                  ffn2_w3[l].astype(BF16), ffn2_w2[l].astype(BF16))
    return xs[None]
```

```python
import functools
import math

import jax
import jax.numpy as jnp
from jax import lax
from jax.experimental import pallas as pl
from jax.experimental.pallas import tpu as pltpu

F32 = jnp.float32
BF16 = jnp.bfloat16

CHUNK = 64
N_HEADS = 8
HEAD_DIM = 64
V_DIM = 128
ROT_DIM = 16
ROPE_THETA = 500000.0
SSM_GROUP = 16
SSM_STATE = 64
EPS = 1e-6
NEG_INF = -1e30
LOG2E = 1.4426950408889634

SSM_T = 16
SSM_W = SSM_T * SSM_GROUP
SCAN_LEVELS = 10

VMEM_LIMIT = 56 * 1024 * 1024


def _cparams(sem):
    return pltpu.CompilerParams(dimension_semantics=sem, vmem_limit_bytes=VMEM_LIMIT)


def _sigmoid(x):
    return 1.0 / (1.0 + jnp.exp(-x))


def _norm_mod(x, g, shift, scale):
    ms = jnp.mean(x * x, axis=-1, keepdims=True)
    y = x * lax.rsqrt(ms + EPS) * g
    return y * (1.0 + scale) + shift


def _ada_kernel(c_ref, w_ref, b_ref, o_ref):
    c = c_ref[...]
    cond = c * _sigmoid(c)
    o_ref[...] = jnp.dot(cond, w_ref[...], preferred_element_type=F32,
                         precision=lax.Precision.HIGHEST) + b_ref[...]


def _ada(c, w_ada, b_ada, tn=1024):
    d, n = w_ada.shape
    c8 = jnp.broadcast_to(c, (8, d))
    out = pl.pallas_call(
        _ada_kernel,
        out_shape=jax.ShapeDtypeStruct((8, n), F32),
        grid=(n // tn,),
        in_specs=[pl.BlockSpec((8, d), lambda j: (0, 0)),
                  pl.BlockSpec((d, tn), lambda j: (0, j)),
                  pl.BlockSpec((1, tn), lambda j: (0, j))],
        out_specs=pl.BlockSpec((8, tn), lambda j: (0, j)),
        compiler_params=_cparams(("arbitrary",)),
        name="ada",
    )(c8, w_ada, b_ada.reshape(1, n))
    return out[0:1]


def _ffn_kernel(x_ref, g_ref, sh_ref, sc_ref, gate_ref, w1_ref, w3_ref, w2_ref,
                o_ref, h_ref, acc_ref):
    f = pl.program_id(1)

    @pl.when(f == 0)
    def _():
        h = _norm_mod(x_ref[...], g_ref[...], sh_ref[...], sc_ref[...])
        h_ref[...] = h.astype(BF16)
        acc_ref[...] = jnp.zeros_like(acc_ref)

    h = h_ref[...]
    a = jnp.dot(h, w1_ref[...], preferred_element_type=F32)
    b = jnp.dot(h, w3_ref[...], preferred_element_type=F32)
    g = (a * _sigmoid(a)) * b
    acc_ref[...] += jnp.dot(g.astype(BF16), w2_ref[...], preferred_element_type=F32)

    @pl.when(f == pl.num_programs(1) - 1)
    def _():
        o_ref[...] = x_ref[...] + (0.5 * gate_ref[...]) * acc_ref[...]


def _ffn(x, g, shift, scale, gate, w1, w3, w2, tm=512, tf=512):
    l, d = x.shape
    dff = w1.shape[1]
    row = lambda i, f: (i, 0)
    vec = lambda i, f: (0, 0)
    return pl.pallas_call(
        _ffn_kernel,
        out_shape=jax.ShapeDtypeStruct((l, d), F32),
        grid=(l // tm, dff // tf),
        in_specs=[pl.BlockSpec((tm, d), row),
                  pl.BlockSpec((1, d), vec), pl.BlockSpec((1, d), vec),
                  pl.BlockSpec((1, d), vec), pl.BlockSpec((1, d), vec),
                  pl.BlockSpec((d, tf), lambda i, f: (0, f)),
                  pl.BlockSpec((d, tf), lambda i, f: (0, f)),
                  pl.BlockSpec((tf, d), lambda i, f: (f, 0))],
        out_specs=pl.BlockSpec((tm, d), row),
        scratch_shapes=[pltpu.VMEM((tm, d), BF16), pltpu.VMEM((tm, d), F32)],
        compiler_params=_cparams(("parallel", "arbitrary")),
        name="ffn",
    )(x, g, shift, scale, gate, w1, w3, w2)


def _proj_kernel(x_ref, g_ref, sh_ref, sc_ref, w_ref, qn_ref, kn_ref, cos_ref, s1_ref,
                 s2_ref, gsum_ref, qt_ref, k_ref, vt_ref, u_ref, h_ref, *, q_scale):
    j = pl.program_id(1)

    @pl.when(j == 0)
    def _():
        h = _norm_mod(x_ref[...], g_ref[...], sh_ref[...], sc_ref[...])
        h_ref[...] = h.astype(BF16)

    p = jnp.dot(h_ref[...], w_ref[...], preferred_element_type=F32)

    def qk_rot(gain_ref, scale, hd):
        pc = p[:, hd * V_DIM:(hd + 1) * V_DIM]
        ss = jnp.dot((pc * pc).astype(BF16), gsum_ref[...], preferred_element_type=F32)
        y = pc * lax.rsqrt(ss * (1.0 / HEAD_DIM) + EPS) * (gain_ref[...] * scale)
        return (y * cos_ref[...] + pltpu.roll(y, V_DIM - ROT_DIM // 2, 1) * s1_ref[...]
                + pltpu.roll(y, ROT_DIM // 2, 1) * s2_ref[...])

    @pl.when(j == 0)
    def _():
        for hd in range(N_HEADS):
            qt_ref[hd, 0] = qk_rot(qn_ref, q_scale, hd).T.astype(BF16)

    @pl.when(j == 1)
    def _():
        for hd in range(N_HEADS):
            k_ref[hd] = qk_rot(kn_ref, 1.0, hd).astype(BF16)

    @pl.when(j == 2)
    def _():
        for hd in range(N_HEADS):
            vt_ref[hd, 0] = p[:, hd * V_DIM:(hd + 1) * V_DIM].T.astype(BF16)

    @pl.when(j == 3)
    def _():
        u_ref[...] = p.astype(BF16)


def _proj(x, g, shift, scale, w_in, qn, kn, cos_t, s1_t, s2_t, gsum, tm=512):
    l, d = x.shape
    tn = 1024
    row = lambda i, j: (i, 0)
    vec = lambda i, j: (0, 0)
    hm = lambda i, j: (0, i, 0)
    tm_t = lambda i, j: (0, i, 0, 0)
    head_major = jax.ShapeDtypeStruct((N_HEADS, l, V_DIM), BF16)
    tiled_t = jax.ShapeDtypeStruct((N_HEADS, l // tm, V_DIM, tm), BF16)
    return pl.pallas_call(
        functools.partial(_proj_kernel, q_scale=HEAD_DIM ** -0.5 * LOG2E),
        out_shape=(tiled_t, head_major, tiled_t, jax.ShapeDtypeStruct((l, tn), BF16)),
        grid=(l // tm, 4),
        in_specs=[pl.BlockSpec((tm, d), row),
                  pl.BlockSpec((1, d), vec), pl.BlockSpec((1, d), vec), pl.BlockSpec((1, d), vec),
                  pl.BlockSpec((d, tn), lambda i, j: (0, j)),
                  pl.BlockSpec((1, V_DIM), vec), pl.BlockSpec((1, V_DIM), vec),
                  pl.BlockSpec((tm, V_DIM), row), pl.BlockSpec((tm, V_DIM), row),
                  pl.BlockSpec((tm, V_DIM), row),
                  pl.BlockSpec((V_DIM, V_DIM), vec)],
        out_specs=(pl.BlockSpec((N_HEADS, 1, V_DIM, tm), tm_t), pl.BlockSpec((N_HEADS, tm, V_DIM), hm),
                   pl.BlockSpec((N_HEADS, 1, V_DIM, tm), tm_t), pl.BlockSpec((tm, tn), row)),
        scratch_shapes=[pltpu.VMEM((tm, d), BF16)],
        compiler_params=_cparams(("parallel", "arbitrary")),
        name="proj",
    )(x, g, shift, scale, w_in, qn, kn, cos_t, s1_t, s2_t, gsum)


ATTN_CW = 256


def _tree8(x, op):
    while x.shape[0] > 8:
        h = x.shape[0] // 2
        x = op(x[:h], x[h:])
    return x


def _attn_kernel(qt_ref, k_ref, vt_ref, cq_ref, ck_ref, lamv_ref, subg_ref, o_ref,
                 q2_ref, m_ref, l_ref, acc_ref, s_buf, p_buf, a_buf, *, tq, lam_init):
    i = pl.program_id(1)
    n_sub = 2 * tq // ATTN_CW
    qt = qt_ref[0, 0]
    feat = lax.broadcasted_iota(jnp.int32, qt.shape, 0)
    zero = jnp.zeros_like(qt)
    q2_ref[:, :tq] = jnp.where(feat < HEAD_DIM, qt, zero)
    q2_ref[:, tq:] = jnp.where(feat >= HEAD_DIM, qt, zero)
    m_ref[...] = jnp.full(m_ref.shape, NEG_INF, F32)
    l_ref[...] = jnp.zeros_like(l_ref)
    acc_ref[...] = jnp.zeros_like(acc_ref)
    p_buf[1] = jnp.zeros(p_buf.shape[1:], BF16)
    a_buf[1] = jnp.zeros(a_buf.shape[1:], F32)

    def cols(c):
        return slice(c * ATTN_CW, (c + 1) * ATTN_CW)

    def qk(j, c):
        kt = k_ref[0, pl.ds(pl.multiple_of(j * tq, tq), tq), :]
        s_buf[c % 2] = jnp.dot(kt, q2_ref[:, cols(c)], preferred_element_type=F32)

    def pv(j, c):
        acc_ref[:, cols(c)] = (a_buf[c % 2] * acc_ref[:, cols(c)]
                               + jnp.dot(vt_ref[0, j], p_buf[c % 2],
                                         preferred_element_type=F32))

    def softmax(c, masked):
        s = s_buf[c % 2]
        if masked:
            q0 = (c * ATTN_CW) % tq
            s = jnp.where(ck_ref[...] <= cq_ref[:, q0:q0 + ATTN_CW], s, NEG_INF)
        m_prev = m_ref[:, cols(c)]
        m_new = jnp.maximum(m_prev, jnp.max(_tree8(s, jnp.maximum), axis=0, keepdims=True))
        alpha = jnp.exp2(m_prev - m_new)
        p = jnp.exp2(s - m_new)
        l_ref[:, cols(c)] = alpha * l_ref[:, cols(c)] + jnp.sum(_tree8(p, jnp.add), axis=0,
                                                                keepdims=True)
        m_ref[:, cols(c)] = m_new
        a_buf[c % 2] = alpha
        p_buf[c % 2] = p.astype(BF16)

    def tile(j, masked):
        for c in range(n_sub):
            if c + 1 < n_sub:
                qk(j, c + 1)
            elif not masked:
                qk(j + 1, 0)
            if c > 0:
                pv(j, c - 1)
            else:
                pv(jnp.maximum(j - 1, 0), n_sub - 1)
            softmax(c, masked)

    def body(j, carry):
        tile(j, False)
        return carry

    qk(0, 0)
    lax.fori_loop(0, i, body, 0)
    tile(i, True)
    pv(i, n_sub - 1)

    o = acc_ref[...] / l_ref[...]
    lv = lamv_ref[...]
    lam = (jnp.exp(jnp.sum(lv[0:1] * lv[1:2], axis=1, keepdims=True))
           - jnp.exp(jnp.sum(lv[2:3] * lv[3:4], axis=1, keepdims=True)) + lam_init)
    od = o[:, :tq] - lam * o[:, tq:]
    ms = jnp.mean(od * od, axis=0, keepdims=True)
    on = od * lax.rsqrt(ms + EPS) * (subg_ref[...] * (1.0 - lam_init))
    o_ref[...] = on.T.astype(BF16)


def _attention(qt, k, vt, cid_col, cid_row, lamv, subg_col, lam_init):
    _, nt, _, tq = qt.shape
    l = nt * tq
    return pl.pallas_call(
        functools.partial(_attn_kernel, tq=tq, lam_init=lam_init),
        out_shape=jax.ShapeDtypeStruct((l, N_HEADS * V_DIM), BF16),
        grid=(N_HEADS, nt),
        in_specs=[pl.BlockSpec((1, 1, V_DIM, tq), lambda h, i: (h, i, 0, 0)),
                  pl.BlockSpec((1, l, V_DIM), lambda h, i: (h, 0, 0)),
                  pl.BlockSpec((1, nt, V_DIM, tq), lambda h, i: (h, 0, 0, 0)),
                  pl.BlockSpec((1, tq), lambda h, i: (0, i)),
                  pl.BlockSpec((tq, 1), lambda h, i: (i, 0)),
                  pl.BlockSpec((4, HEAD_DIM), lambda h, i: (0, 0)),
                  pl.BlockSpec((V_DIM, 1), lambda h, i: (0, 0))],
        out_specs=pl.BlockSpec((tq, V_DIM), lambda h, i: (i, h)),
        scratch_shapes=[pltpu.VMEM((V_DIM, 2 * tq), BF16), pltpu.VMEM((1, 2 * tq), F32),
                        pltpu.VMEM((1, 2 * tq), F32), pltpu.VMEM((V_DIM, 2 * tq), F32),
                        pltpu.VMEM((2, tq, ATTN_CW), F32), pltpu.VMEM((2, tq, ATTN_CW), BF16),
                        pltpu.VMEM((2, 1, ATTN_CW), F32)],
        compiler_params=_cparams(("parallel", "arbitrary")),
        name="attn",
    )(qt, k, vt, cid_row, cid_col, lamv, subg_col)


def _ssm_kernel(u_ref, at_ref, bm_ref, cm_ref, pr_ref, pi_ref, y_ref):
    u = u_ref[0]
    nb = u.shape[0]
    x = jnp.dot(u, bm_ref[0], preferred_element_type=F32)
    rows = lax.broadcasted_iota(jnp.int32, x.shape, 0)

    def shift_down(z, d):
        if d % 8 == 0:
            return jnp.concatenate([jnp.zeros((d, z.shape[1]), F32), z[:nb - d]], axis=0)
        return jnp.where(rows >= d, pltpu.roll(z, d, 0), 0.0)

    pr, pi = pr_ref[0], pi_ref[0]
    for lvl in range(SCAN_LEVELS):
        z = shift_down(x, 1 << lvl)
        x = x + pr[lvl:lvl + 1] * z + pi[lvl:lvl + 1] * pltpu.roll(z, SSM_STATE, 1)
    xprev = shift_down(x, 1)
    y = jnp.dot(u, at_ref[0], preferred_element_type=F32)
    y = y + jnp.dot(xprev.astype(BF16), cm_ref[0], preferred_element_type=F32)
    y_ref[0] = y


def _ssm(u_blk, at, bm, cm, pr, pi):
    g, nb, w = u_blk.shape
    blk = lambda a: pl.BlockSpec((1,) + a.shape[1:], lambda i: (i, 0, 0))
    return pl.pallas_call(
        _ssm_kernel,
        out_shape=jax.ShapeDtypeStruct((g, nb, w), F32),
        grid=(g,),
        in_specs=[blk(u_blk), blk(at), blk(bm), blk(cm), blk(pr), blk(pi)],
        out_specs=pl.BlockSpec((1, nb, w), lambda i: (i, 0, 0)),
        compiler_params=_cparams(("parallel",)),
        name="ssm",
    )(u_blk, at, bm, cm, pr, pi)


def _ssm_prep(a_re, a_im, log_dt, b_re, b_im, c_re, c_im, d_skip):
    hp = lax.Precision.HIGHEST
    lr = jnp.minimum(a_re, -1e-4)
    li = a_im
    dt = jnp.exp(log_dt)[:, None]

    def lpow(n):
        n = jnp.asarray(n, F32)[..., None, None]
        mag = jnp.exp(n * (lr * dt))
        ang = n * (li * dt)
        return mag * jnp.cos(ang), mag * jnp.sin(ang)

    th = li * dt
    nr = jnp.expm1(lr * dt) * jnp.cos(th) - 2.0 * jnp.sin(0.5 * th) ** 2
    ni = jnp.exp(lr * dt) * jnp.sin(th)
    den = lr * lr + li * li
    fr = (nr * lr + ni * li) / den
    fi = (ni * lr - nr * li) / den
    bbr = fr[..., None] * b_re - fi[..., None] * b_im
    bbi = fr[..., None] * b_im + fi[..., None] * b_re

    t = jnp.arange(SSM_T)
    pjr, pji = lpow(t)
    mr = pjr[..., None] * bbr[None] - pji[..., None] * bbi[None]
    mi = pjr[..., None] * bbi[None] + pji[..., None] * bbr[None]
    kj = (jnp.einsum('gcp,jgpd->gjcd', c_re, mr, precision=hp)
          - jnp.einsum('gcp,jgpd->gjcd', c_im, mi, precision=hp))
    lag = t[None, :] - t[:, None]
    kt = kj[:, jnp.clip(lag, 0, SSM_T - 1)]
    kt = jnp.where((lag >= 0)[None, :, :, None, None], kt, 0.0)
    eye_t = jnp.eye(SSM_T, dtype=F32)
    eye_c = jnp.eye(SSM_GROUP, dtype=F32)
    kt = kt + (eye_t[None, :, :, None, None] * eye_c[None, None, None]
               * d_skip[:, None, None, :, None])
    g = a_re.shape[0]
    at = kt.transpose(0, 1, 4, 2, 3).reshape(g, SSM_W, SSM_W)

    rr, ri = lpow(SSM_T - 1 - t)
    sr = rr[..., None] * bbr[None] - ri[..., None] * bbi[None]
    si = rr[..., None] * bbi[None] + ri[..., None] * bbr[None]
    bm = jnp.concatenate([sr, si], axis=2)
    bm = bm.transpose(1, 0, 3, 2).reshape(g, SSM_W, 2 * SSM_STATE)

    qr, qi = lpow(t + 1)
    wr = c_re[None] * qr[:, :, None, :] - c_im[None] * qi[:, :, None, :]
    wi = c_re[None] * qi[:, :, None, :] + c_im[None] * qr[:, :, None, :]
    cm = jnp.concatenate([wr, -wi], axis=3)
    cm = cm.transpose(1, 3, 0, 2).reshape(g, 2 * SSM_STATE, SSM_W)

    er, ei = lpow(SSM_T * (2 ** jnp.arange(16)))
    pr = jnp.concatenate([er, er], axis=2).transpose(1, 0, 2)
    pi = jnp.concatenate([-ei, ei], axis=2).transpose(1, 0, 2)
    return at.astype(BF16), bm.astype(BF16), cm.astype(BF16), pr, pi


def _post_kernel(x_ref, attn_ref, y_ref, wglu_ref, bglu_ref, sn_ref, woa_ref, wob_ref,
                 g2_ref, o_ref):
    y = jax.nn.gelu(y_ref[...], approximate=True)
    z = jnp.dot(y.astype(BF16), wglu_ref[...], preferred_element_type=F32) + bglu_ref[...]
    y = y * _sigmoid(z)
    ms = jnp.mean(y * y, axis=-1, keepdims=True)
    y = y * lax.rsqrt(ms + EPS) * sn_ref[...]
    mixed = (jnp.dot(attn_ref[...], woa_ref[...], preferred_element_type=F32)
             + jnp.dot(y.astype(BF16), wob_ref[...], preferred_element_type=F32))
    o_ref[...] = x_ref[...] + g2_ref[...] * mixed


def _post(x, attn, y, w_glu, b_glu, sn, w_out, g2, tm=256):
    l, d = x.shape
    w = y.shape[1]
    row = lambda i: (i, 0)
    vec = lambda i: (0, 0)
    return pl.pallas_call(
        _post_kernel,
        out_shape=jax.ShapeDtypeStruct((l, d), F32),
        grid=(l // tm,),
        in_specs=[pl.BlockSpec((tm, d), row), pl.BlockSpec((tm, w), row), pl.BlockSpec((tm, w), row),
                  pl.BlockSpec((w, w), vec), pl.BlockSpec((1, w), vec), pl.BlockSpec((1, w), vec),
                  pl.BlockSpec((w, d), lambda i: (0, 0)), pl.BlockSpec((w, d), lambda i: (1, 0)),
                  pl.BlockSpec((1, d), vec)],
        out_specs=pl.BlockSpec((tm, d), row),
        compiler_params=_cparams(("parallel",)),
        name="post",
    )(x, attn, y, w_glu, b_glu, sn, w_out, w_out, g2)


def _rope_tables(positions):
    inv_freq = ROPE_THETA ** (-jnp.arange(0, ROT_DIM, 2, dtype=F32) / ROT_DIM)
    ang = positions.astype(F32)[:, None] * inv_freq
    cos, sin = jnp.cos(ang), jnp.sin(ang)
    l = positions.shape[0]
    half = ROT_DIM // 2
    ones = jnp.ones((l, HEAD_DIM - ROT_DIM), F32)
    zeros = jnp.zeros((l, HEAD_DIM - ROT_DIM), F32)
    zh = jnp.zeros((l, half), F32)
    cos_t = jnp.concatenate([cos, cos, ones], axis=1)
    s1_t = jnp.concatenate([-sin, zh, zeros], axis=1)
    s2_t = jnp.concatenate([zh, sin, zeros], axis=1)
    tile2 = lambda a: jnp.concatenate([a, a], axis=1)
    return tile2(cos_t), tile2(s1_t), tile2(s2_t)


def kernel(x, c, positions, w_ada, b_ada, ffn1_norm, ffn1_w1, ffn1_w3, ffn1_w2, mix_norm, w_in, q_norm, k_norm, lambda_q1, lambda_k1, lambda_q2, lambda_k2, attn_subln, ssm_a_re, ssm_a_im, ssm_log_dt, ssm_b_re, ssm_b_im, ssm_c_re, ssm_c_im, ssm_d, w_glu, b_glu, ssm_out_norm, w_out, ffn2_norm, ffn2_w1, ffn2_w3, ffn2_w2):
    batch, seq, d = x.shape
    depth = w_ada.shape[0]
    assert batch == 1 and seq == SSM_T * 2 ** SCAN_LEVELS
    pos = positions[0]
    cid = pos // CHUNK
    cid_col, cid_row = cid.reshape(seq, 1), cid.reshape(1, seq)
    cos_t, s1_t, s2_t = _rope_tables(pos)
    lane = jnp.arange(V_DIM)
    gsum = (lane[:, None] // HEAD_DIM == lane[None, :] // HEAD_DIM).astype(BF16)
    tile2 = lambda a: jnp.concatenate([a, a]).reshape(1, V_DIM)
    nb = seq // SSM_T
    n_groups = ssm_a_re.shape[1]

    xs = x[0]
    for l in range(depth):
        lam_init = 0.8 - 0.6 * math.exp(-0.3 * l)
        mod = _ada(c, w_ada[l], b_ada[l])
        sh1, sc1, g1, sh2, sc2, g2, sh3, sc3, g3 = jnp.split(mod, 9, axis=-1)
        vec = lambda a: a.reshape(1, -1)

        xs = _ffn(xs, vec(ffn1_norm[l]), sh1, sc1, g1, ffn1_w1[l].astype(BF16),
                  ffn1_w3[l].astype(BF16), ffn1_w2[l].astype(BF16))

        qt, k, vt, u = _proj(xs, vec(mix_norm[l]), sh2, sc2, w_in[l].astype(BF16),
                             tile2(q_norm[l]), tile2(k_norm[l]), cos_t, s1_t, s2_t, gsum)

        lamv = jnp.stack([lambda_q1[l], lambda_k1[l], lambda_q2[l], lambda_k2[l]]).astype(F32)
        attn = _attention(qt, k, vt, cid_col, cid_row, lamv,
                          attn_subln[l].astype(F32).reshape(V_DIM, 1), lam_init)

        at, bm, cm, pr, pi = _ssm_prep(ssm_a_re[l], ssm_a_im[l], ssm_log_dt[l], ssm_b_re[l],
                                       ssm_b_im[l], ssm_c_re[l], ssm_c_im[l], ssm_d[l])
        u_blk = (u.reshape(nb, SSM_T, n_groups, SSM_GROUP).transpose(2, 0, 1, 3)
                 .reshape(n_groups, nb, SSM_W))
        y_blk = _ssm(u_blk, at, bm, cm, pr, pi)
        y = (y_blk.reshape(n_groups, nb, SSM_T, SSM_GROUP).transpose(1, 2, 0, 3)
             .reshape(seq, n_groups * SSM_GROUP))

        xs = _post(xs, attn, y, w_glu[l].astype(BF16), vec(b_glu[l]), vec(ssm_out_norm[l]),
                   w_out[l].astype(BF16), g2)

        xs = _ffn(xs, vec(ffn2_norm[l]), sh3, sc3, g3, ffn2_w1[l].astype(BF16),
                  ffn2_w3[l].astype(BF16), ffn2_w2[l].astype(BF16))
    return xs[None]
```

```python
import functools
import math

import jax
import jax.numpy as jnp
from jax import lax
from jax.experimental import pallas as pl
from jax.experimental.pallas import tpu as pltpu

F32 = jnp.float32
BF16 = jnp.bfloat16

CHUNK = 64
N_HEADS = 8
HEAD_DIM = 64
V_DIM = 128
ROT_DIM = 16
ROPE_THETA = 500000.0
SSM_GROUP = 16
SSM_STATE = 64
EPS = 1e-6
NEG_INF = -1e30
LOG2E = 1.4426950408889634

SSM_T = 16
SSM_W = SSM_T * SSM_GROUP

VMEM_LIMIT = 56 * 1024 * 1024


def _cparams(sem):
    return pltpu.CompilerParams(dimension_semantics=sem, vmem_limit_bytes=VMEM_LIMIT)


def _sigmoid(x):
    return 1.0 / (1.0 + jnp.exp(-x))


def _norm_mod(x, g, shift, scale):
    ms = jnp.mean(x * x, axis=-1, keepdims=True)
    y = x * lax.rsqrt(ms + EPS) * g
    return y * (1.0 + scale) + shift


def _ada_kernel(c_ref, w_ref, b_ref, o_ref):
    c = c_ref[...]
    cond = c * _sigmoid(c)
    o_ref[...] = jnp.dot(cond, w_ref[...], preferred_element_type=F32,
                         precision=lax.Precision.HIGHEST) + b_ref[...]


def _ada(c, w_ada, b_ada, tn=1024):
    d, n = w_ada.shape
    c8 = jnp.broadcast_to(c, (8, d))
    out = pl.pallas_call(
        _ada_kernel,
        out_shape=jax.ShapeDtypeStruct((8, n), F32),
        grid=(n // tn,),
        in_specs=[pl.BlockSpec((8, d), lambda j: (0, 0)),
                  pl.BlockSpec((d, tn), lambda j: (0, j)),
                  pl.BlockSpec((1, tn), lambda j: (0, j))],
        out_specs=pl.BlockSpec((8, tn), lambda j: (0, j)),
        compiler_params=_cparams(("arbitrary",)),
        name="ada",
    )(c8, w_ada, b_ada.reshape(1, n))
    return out[0:1]


def _ffn_kernel(x_ref, g_ref, sh_ref, sc_ref, gate_ref, w1_ref, w3_ref, w2_ref,
                o_ref, h_ref, acc_ref):
    f = pl.program_id(1)

    @pl.when(f == 0)
    def _():
        h = _norm_mod(x_ref[...], g_ref[...], sh_ref[...], sc_ref[...])
        h_ref[...] = h.astype(BF16)
        acc_ref[...] = jnp.zeros_like(acc_ref)

    h = h_ref[...]
    a = jnp.dot(h, w1_ref[...], preferred_element_type=F32)
    b = jnp.dot(h, w3_ref[...], preferred_element_type=F32)
    g = (a * _sigmoid(a)) * b
    acc_ref[...] += jnp.dot(g.astype(BF16), w2_ref[...], preferred_element_type=F32)

    @pl.when(f == pl.num_programs(1) - 1)
    def _():
        o_ref[...] = x_ref[...] + (0.5 * gate_ref[...]) * acc_ref[...]


def _ffn(x, g, shift, scale, gate, w1, w3, w2, tm=512, tf=512):
    l, d = x.shape
    dff = w1.shape[1]
    row = lambda i, f: (i, 0)
    vec = lambda i, f: (0, 0)
    return pl.pallas_call(
        _ffn_kernel,
        out_shape=jax.ShapeDtypeStruct((l, d), F32),
        grid=(l // tm, dff // tf),
        in_specs=[pl.BlockSpec((tm, d), row),
                  pl.BlockSpec((1, d), vec), pl.BlockSpec((1, d), vec),
                  pl.BlockSpec((1, d), vec), pl.BlockSpec((1, d), vec),
                  pl.BlockSpec((d, tf), lambda i, f: (0, f)),
                  pl.BlockSpec((d, tf), lambda i, f: (0, f)),
                  pl.BlockSpec((tf, d), lambda i, f: (f, 0))],
        out_specs=pl.BlockSpec((tm, d), row),
        scratch_shapes=[pltpu.VMEM((tm, d), BF16), pltpu.VMEM((tm, d), F32)],
        compiler_params=_cparams(("parallel", "arbitrary")),
        name="ffn",
    )(x, g, shift, scale, gate, w1, w3, w2)


def _proj_kernel(x_ref, g_ref, sh_ref, sc_ref, w_ref, qn_ref, kn_ref, cos_ref, s1_ref,
                 s2_ref, gsum_ref, qt_ref, k_ref, vt_ref, u_ref, h_ref, *, q_scale):
    j = pl.program_id(1)

    @pl.when(j == 0)
    def _():
        h = _norm_mod(x_ref[...], g_ref[...], sh_ref[...], sc_ref[...])
        h_ref[...] = h.astype(BF16)

    p = jnp.dot(h_ref[...], w_ref[...], preferred_element_type=F32)

    def qk_rot(gain_ref, scale, hd):
        pc = p[:, hd * V_DIM:(hd + 1) * V_DIM]
        ss = jnp.dot((pc * pc).astype(BF16), gsum_ref[...], preferred_element_type=F32)
        y = pc * lax.rsqrt(ss * (1.0 / HEAD_DIM) + EPS) * (gain_ref[...] * scale)
        return (y * cos_ref[...] + pltpu.roll(y, V_DIM - ROT_DIM // 2, 1) * s1_ref[...]
                + pltpu.roll(y, ROT_DIM // 2, 1) * s2_ref[...])

    @pl.when(j == 0)
    def _():
        for hd in range(N_HEADS):
            qt_ref[hd, 0] = qk_rot(qn_ref, q_scale, hd).T.astype(BF16)

    @pl.when(j == 1)
    def _():
        for hd in range(N_HEADS):
            k_ref[hd] = qk_rot(kn_ref, 1.0, hd).astype(BF16)

    @pl.when(j == 2)
    def _():
        for hd in range(N_HEADS):
            vt_ref[hd, 0] = p[:, hd * V_DIM:(hd + 1) * V_DIM].T.astype(BF16)

    @pl.when(j == 3)
    def _():
        u_ref[...] = p.astype(BF16)


def _proj(x, g, shift, scale, w_in, qn, kn, cos_t, s1_t, s2_t, gsum, tm=512):
    l, d = x.shape
    tn = 1024
    row = lambda i, j: (i, 0)
    vec = lambda i, j: (0, 0)
    hm = lambda i, j: (0, i, 0)
    tm_t = lambda i, j: (0, i, 0, 0)
    head_major = jax.ShapeDtypeStruct((N_HEADS, l, V_DIM), BF16)
    tiled_t = jax.ShapeDtypeStruct((N_HEADS, l // tm, V_DIM, tm), BF16)
    return pl.pallas_call(
        functools.partial(_proj_kernel, q_scale=HEAD_DIM ** -0.5 * LOG2E),
        out_shape=(tiled_t, head_major, tiled_t, jax.ShapeDtypeStruct((l, tn), BF16)),
        grid=(l // tm, 4),
        in_specs=[pl.BlockSpec((tm, d), row),
                  pl.BlockSpec((1, d), vec), pl.BlockSpec((1, d), vec), pl.BlockSpec((1, d), vec),
                  pl.BlockSpec((d, tn), lambda i, j: (0, j)),
                  pl.BlockSpec((1, V_DIM), vec), pl.BlockSpec((1, V_DIM), vec),
                  pl.BlockSpec((tm, V_DIM), row), pl.BlockSpec((tm, V_DIM), row),
                  pl.BlockSpec((tm, V_DIM), row),
                  pl.BlockSpec((V_DIM, V_DIM), vec)],
        out_specs=(pl.BlockSpec((N_HEADS, 1, V_DIM, tm), tm_t), pl.BlockSpec((N_HEADS, tm, V_DIM), hm),
                   pl.BlockSpec((N_HEADS, 1, V_DIM, tm), tm_t), pl.BlockSpec((tm, tn), row)),
        scratch_shapes=[pltpu.VMEM((tm, d), BF16)],
        compiler_params=_cparams(("parallel", "arbitrary")),
        name="proj",
    )(x, g, shift, scale, w_in, qn, kn, cos_t, s1_t, s2_t, gsum)


ATTN_CW = 256
ATTN_BOUND_SLACK = 1.02
ATTN_BOUND_MAX = 48.0


def _tree8(x, op):
    while x.shape[0] > 8:
        h = x.shape[0] // 2
        x = op(x[:h], x[h:])
    return x


def _attn_kernel(qt_ref, k_ref, vt_ref, cq_ref, ck_ref, lamv_ref, subg_ref, o_ref,
                 q2_ref, m_ref, l_ref, acc_ref, s_buf, p_buf, a_buf, kn_ref, b_ref, l8_ref,
                 *, tq, lam_init):
    i = pl.program_id(1)
    n_sub = 2 * tq // ATTN_CW
    n_tiles = k_ref.shape[1] // tq

    @pl.when(i == 0)
    def _():
        ones = jnp.ones((V_DIM, V_DIM), BF16)

        def chunk(r, mx):
            kk = k_ref[0, pl.ds(pl.multiple_of(r * tq, tq), tq), :].astype(F32)
            n2 = jnp.dot((kk * kk).astype(BF16), ones, preferred_element_type=F32)
            return jnp.maximum(mx, jnp.max(_tree8(n2, jnp.maximum), axis=0, keepdims=True))

        kn_ref[...] = lax.fori_loop(0, n_tiles, chunk, jnp.zeros(kn_ref.shape, F32))

    qt = qt_ref[0, 0]
    feat = lax.broadcasted_iota(jnp.int32, qt.shape, 0)
    zero = jnp.zeros_like(qt)
    q2_ref[:, :tq] = jnp.where(feat < HEAD_DIM, qt, zero)
    q2_ref[:, tq:] = jnp.where(feat >= HEAD_DIM, qt, zero)
    acc_ref[...] = jnp.zeros_like(acc_ref)
    p_buf[1] = jnp.zeros(p_buf.shape[1:], BF16)

    q2f = q2_ref[...].astype(F32)
    qn2 = jnp.sum(_tree8(q2f * q2f, jnp.add), axis=0, keepdims=True)
    kn2 = jnp.concatenate([kn_ref[...]] * (2 * tq // V_DIM), axis=1)
    bound = jnp.sqrt(qn2 * kn2) * ATTN_BOUND_SLACK
    b_ref[...] = bound
    bounded_ok = jnp.max(bound) <= ATTN_BOUND_MAX

    def cols(c):
        return slice(c * ATTN_CW, (c + 1) * ATTN_CW)

    def key_tile(j):
        return k_ref[0, pl.ds(pl.multiple_of(j * tq, tq), tq), :]

    def mask(c, s):
        q0 = (c * ATTN_CW) % tq
        return jnp.where(ck_ref[...] <= cq_ref[:, q0:q0 + ATTN_CW], s, NEG_INF)

    def pipeline(tile, last_pv, pairs):
        if pairs:
            odd = i % 2

            @pl.when(odd == 1)
            def _():
                tile(0, False)

            def body(jj, carry):
                tile(odd + 2 * jj, False)
                tile(odd + 2 * jj + 1, False)
                return carry

            lax.fori_loop(0, i // 2, body, 0)
        else:
            def body(j, carry):
                tile(j, False)
                return carry

            lax.fori_loop(0, i, body, 0)
        tile(i, True)
        last_pv(i, n_sub - 1)

    def qk(j, c):
        s_buf[c % 2] = jnp.dot(key_tile(j), q2_ref[:, cols(c)], preferred_element_type=F32)

    def stages(j, masked, pv_fn, softmax_fn):
        for c in range(n_sub):
            if c + 1 < n_sub:
                qk(j, c + 1)
            elif not masked:
                qk(j + 1, 0)
            if c > 0:
                pv_fn(j, c - 1)
            else:
                pv_fn(jnp.maximum(j - 1, 0), n_sub - 1)
            softmax_fn(c, masked)

    def b_softmax(c, masked):
        s = s_buf[c % 2]
        if masked:
            s = mask(c, s)
        p = jnp.exp2(s - b_ref[:, cols(c)])
        l8_ref[:, cols(c)] += _tree8(p, jnp.add)
        p_buf[c % 2] = p.astype(BF16)

    def b_pv(j, c):
        acc_ref[:, cols(c)] += jnp.dot(vt_ref[0, j], p_buf[c % 2], preferred_element_type=F32)

    @pl.when(bounded_ok)
    def _():
        l8_ref[...] = jnp.zeros_like(l8_ref)
        qk(0, 0)
        pipeline(functools.partial(stages, pv_fn=b_pv, softmax_fn=b_softmax), b_pv, pairs=True)
        l_ref[...] = jnp.sum(l8_ref[...], axis=0, keepdims=True)

    def pv(j, c):
        acc_ref[:, cols(c)] = (a_buf[c % 2] * acc_ref[:, cols(c)]
                               + jnp.dot(vt_ref[0, j], p_buf[c % 2],
                                         preferred_element_type=F32))

    def softmax(c, masked):
        s = s_buf[c % 2]
        if masked:
            s = mask(c, s)
        m_prev = m_ref[:, cols(c)]
        m_new = jnp.maximum(m_prev, jnp.max(_tree8(s, jnp.maximum), axis=0, keepdims=True))
        alpha = jnp.exp2(m_prev - m_new)
        p = jnp.exp2(s - m_new)
        l_ref[:, cols(c)] = alpha * l_ref[:, cols(c)] + jnp.sum(_tree8(p, jnp.add), axis=0,
                                                                keepdims=True)
        m_ref[:, cols(c)] = m_new
        a_buf[c % 2] = alpha
        p_buf[c % 2] = p.astype(BF16)

    @pl.when(jnp.logical_not(bounded_ok))
    def _():
        m_ref[...] = jnp.full(m_ref.shape, NEG_INF, F32)
        l_ref[...] = jnp.zeros_like(l_ref)
        a_buf[1] = jnp.zeros(a_buf.shape[1:], F32)
        qk(0, 0)
        pipeline(functools.partial(stages, pv_fn=pv, softmax_fn=softmax), pv, pairs=False)

    o = acc_ref[...] / l_ref[...]
    lv = lamv_ref[...]
    lam = (jnp.exp(jnp.sum(lv[0:1] * lv[1:2], axis=1, keepdims=True))
           - jnp.exp(jnp.sum(lv[2:3] * lv[3:4], axis=1, keepdims=True)) + lam_init)
    od = o[:, :tq] - lam * o[:, tq:]
    ms = jnp.mean(od * od, axis=0, keepdims=True)
    on = od * lax.rsqrt(ms + EPS) * (subg_ref[...] * (1.0 - lam_init))
    o_ref[...] = on.T.astype(BF16)


def _attention(qt, k, vt, cid_col, cid_row, lamv, subg_col, lam_init):
    _, nt, _, tq = qt.shape
    l = nt * tq
    return pl.pallas_call(
        functools.partial(_attn_kernel, tq=tq, lam_init=lam_init),
        out_shape=jax.ShapeDtypeStruct((l, N_HEADS * V_DIM), BF16),
        grid=(N_HEADS, nt),
        in_specs=[pl.BlockSpec((1, 1, V_DIM, tq), lambda h, i: (h, i, 0, 0)),
                  pl.BlockSpec((1, l, V_DIM), lambda h, i: (h, 0, 0)),
                  pl.BlockSpec((1, nt, V_DIM, tq), lambda h, i: (h, 0, 0, 0)),
                  pl.BlockSpec((1, tq), lambda h, i: (0, i)),
                  pl.BlockSpec((tq, 1), lambda h, i: (i, 0)),
                  pl.BlockSpec((4, HEAD_DIM), lambda h, i: (0, 0)),
                  pl.BlockSpec((V_DIM, 1), lambda h, i: (0, 0))],
        out_specs=pl.BlockSpec((tq, V_DIM), lambda h, i: (i, h)),
        scratch_shapes=[pltpu.VMEM((V_DIM, 2 * tq), BF16), pltpu.VMEM((1, 2 * tq), F32),
                        pltpu.VMEM((1, 2 * tq), F32), pltpu.VMEM((V_DIM, 2 * tq), F32),
                        pltpu.VMEM((2, tq, ATTN_CW), F32), pltpu.VMEM((2, tq, ATTN_CW), BF16),
                        pltpu.VMEM((2, 1, ATTN_CW), F32), pltpu.VMEM((1, V_DIM), F32),
                        pltpu.VMEM((1, 2 * tq), F32), pltpu.VMEM((8, 2 * tq), F32)],
        compiler_params=_cparams(("parallel", "arbitrary")),
        name="attn",
    )(qt, k, vt, cid_row, cid_col, lamv, subg_col)


SSM_OCT = 8
SSM_CB = 256


def _ssm_kernel(u_ref, at_ref, bm_ref, cm_ref, pr_ref, pi_ref, y_ref, carry_ref):
    @pl.when(pl.program_id(1) == 0)
    def _():
        carry_ref[...] = jnp.zeros_like(carry_ref)

    u = u_ref[0]
    cb = u.shape[0]
    half = carry_ref.shape[1] // 2
    pr, pi = pr_ref[0], pi_ref[0]

    def cmul(lvl, z):
        return pr[lvl:lvl + 1] * z + pi[lvl:lvl + 1] * pltpu.roll(z, half, 1)

    x = jnp.dot(u, bm_ref[0], preferred_element_type=F32)
    rows = lax.broadcasted_iota(jnp.int32, x.shape, 0)
    carry = carry_ref[...]
    x = x + jnp.where(rows == 0, cmul(0, carry), 0.0)

    def shift_down(z, d):
        if d % 8 == 0:
            return jnp.concatenate([jnp.zeros((d, z.shape[1]), F32), z[:cb - d]], axis=0)
        return jnp.where(rows >= d, pltpu.roll(z, d, 0), 0.0)

    for lvl in range(cb.bit_length() - 1):
        x = x + cmul(lvl, shift_down(x, 1 << lvl))
    carry_ref[...] = x[cb - 1:cb]
    xprev = shift_down(x, 1) + jnp.where(rows == 0, carry, 0.0)
    y = jnp.dot(u, at_ref[0], preferred_element_type=F32)
    y_ref[0] = y + jnp.dot(xprev.astype(BF16), cm_ref[0], preferred_element_type=F32)


def _ssm(u_oct, at, bm, cm, pr, pi):
    n_oct, nb, w = u_oct.shape
    ns = bm.shape[2]
    per_oct = lambda a: pl.BlockSpec((1,) + a.shape[1:], lambda o, t: (o, 0, 0))
    return pl.pallas_call(
        _ssm_kernel,
        out_shape=jax.ShapeDtypeStruct((n_oct, nb, w), F32),
        grid=(n_oct, nb // SSM_CB),
        in_specs=[pl.BlockSpec((1, SSM_CB, w), lambda o, t: (o, t, 0)),
                  per_oct(at), per_oct(bm), per_oct(cm), per_oct(pr), per_oct(pi)],
        out_specs=pl.BlockSpec((1, SSM_CB, w), lambda o, t: (o, t, 0)),
        scratch_shapes=[pltpu.VMEM((1, ns), F32)],
        compiler_params=_cparams(("parallel", "arbitrary")),
        name="ssm",
    )(u_oct, at, bm, cm, pr, pi)


def _ssm_octets(at, bm, cm, pr, pi):
    g = at.shape[0]
    no, t, c, p = g // SSM_OCT, SSM_T, SSM_GROUP, SSM_STATE
    eye = jnp.eye(SSM_OCT, dtype=BF16)
    at, bm, cm = at.astype(BF16), bm.astype(BF16), cm.astype(BF16)
    at_o = jnp.einsum('Ggsdtc,gk->Gsgdtkc', at.reshape(no, SSM_OCT, t, c, t, c), eye)
    at_o = at_o.reshape(no, t * SSM_OCT * c, t * SSM_OCT * c)
    bm_o = jnp.einsum('Ggsdrp,gk->Gsgdrkp', bm.reshape(no, SSM_OCT, t, c, 2, p), eye)
    bm_o = bm_o.reshape(no, t * SSM_OCT * c, 2 * SSM_OCT * p)
    cm_o = jnp.einsum('Ggrptc,gk->Grgptkc', cm.reshape(no, SSM_OCT, 2, p, t, c), eye)
    cm_o = cm_o.reshape(no, 2 * SSM_OCT * p, t * SSM_OCT * c)
    lanes = lambda a: (a.reshape(no, SSM_OCT, 16, 2, p).transpose(0, 2, 3, 1, 4)
                       .reshape(no, 16, 2 * SSM_OCT * p))
    return at_o, bm_o, cm_o, lanes(pr), lanes(pi)


def _ssm_prep(a_re, a_im, log_dt, b_re, b_im, c_re, c_im, d_skip):
    hp = lax.Precision.HIGHEST
    lr = jnp.minimum(a_re, -1e-4)
    li = a_im
    dt = jnp.exp(log_dt)[:, None]

    def lpow(n):
        n = jnp.asarray(n, F32)[..., None, None]
        mag = jnp.exp(n * (lr * dt))
        ang = n * (li * dt)
        return mag * jnp.cos(ang), mag * jnp.sin(ang)

    th = li * dt
    nr = jnp.expm1(lr * dt) * jnp.cos(th) - 2.0 * jnp.sin(0.5 * th) ** 2
    ni = jnp.exp(lr * dt) * jnp.sin(th)
    den = lr * lr + li * li
    fr = (nr * lr + ni * li) / den
    fi = (ni * lr - nr * li) / den
    bbr = fr[..., None] * b_re - fi[..., None] * b_im
    bbi = fr[..., None] * b_im + fi[..., None] * b_re

    t = jnp.arange(SSM_T)
    pjr, pji = lpow(t)
    mr = pjr[..., None] * bbr[None] - pji[..., None] * bbi[None]
    mi = pjr[..., None] * bbi[None] + pji[..., None] * bbr[None]
    kj = (jnp.einsum('gcp,jgpd->gjcd', c_re, mr, precision=hp)
          - jnp.einsum('gcp,jgpd->gjcd', c_im, mi, precision=hp))
    lag = t[None, :] - t[:, None]
    kt = kj[:, jnp.clip(lag, 0, SSM_T - 1)]
    kt = jnp.where((lag >= 0)[None, :, :, None, None], kt, 0.0)
    eye_t = jnp.eye(SSM_T, dtype=F32)
    eye_c = jnp.eye(SSM_GROUP, dtype=F32)
    kt = kt + (eye_t[None, :, :, None, None] * eye_c[None, None, None]
               * d_skip[:, None, None, :, None])
    g = a_re.shape[0]
    at = kt.transpose(0, 1, 4, 2, 3).reshape(g, SSM_W, SSM_W)

    rr, ri = lpow(SSM_T - 1 - t)
    sr = rr[..., None] * bbr[None] - ri[..., None] * bbi[None]
    si = rr[..., None] * bbi[None] + ri[..., None] * bbr[None]
    bm = jnp.concatenate([sr, si], axis=2)
    bm = bm.transpose(1, 0, 3, 2).reshape(g, SSM_W, 2 * SSM_STATE)

    qr, qi = lpow(t + 1)
    wr = c_re[None] * qr[:, :, None, :] - c_im[None] * qi[:, :, None, :]
    wi = c_re[None] * qi[:, :, None, :] + c_im[None] * qr[:, :, None, :]
    cm = jnp.concatenate([wr, -wi], axis=3)
    cm = cm.transpose(1, 3, 0, 2).reshape(g, 2 * SSM_STATE, SSM_W)

    er, ei = lpow(SSM_T * (2 ** jnp.arange(16)))
    pr = jnp.concatenate([er, er], axis=2).transpose(1, 0, 2)
    pi = jnp.concatenate([-ei, ei], axis=2).transpose(1, 0, 2)
    return at, bm, cm, pr, pi


def _post_kernel(x_ref, attn_ref, y_ref, wglu_ref, bglu_ref, sn_ref, woa_ref, wob_ref,
                 g2_ref, o_ref):
    y = jax.nn.gelu(y_ref[...], approximate=True)
    z = jnp.dot(y.astype(BF16), wglu_ref[...], preferred_element_type=F32) + bglu_ref[...]
    y = y * _sigmoid(z)
    ms = jnp.mean(y * y, axis=-1, keepdims=True)
    y = y * lax.rsqrt(ms + EPS) * sn_ref[...]
    mixed = (jnp.dot(attn_ref[...], woa_ref[...], preferred_element_type=F32)
             + jnp.dot(y.astype(BF16), wob_ref[...], preferred_element_type=F32))
    o_ref[...] = x_ref[...] + g2_ref[...] * mixed


def _post(x, attn, y, w_glu, b_glu, sn, w_out, g2, tm=256):
    l, d = x.shape
    w = y.shape[1]
    row = lambda i: (i, 0)
    vec = lambda i: (0, 0)
    return pl.pallas_call(
        _post_kernel,
        out_shape=jax.ShapeDtypeStruct((l, d), F32),
        grid=(l // tm,),
        in_specs=[pl.BlockSpec((tm, d), row), pl.BlockSpec((tm, w), row), pl.BlockSpec((tm, w), row),
                  pl.BlockSpec((w, w), vec), pl.BlockSpec((1, w), vec), pl.BlockSpec((1, w), vec),
                  pl.BlockSpec((w, d), lambda i: (0, 0)), pl.BlockSpec((w, d), lambda i: (1, 0)),
                  pl.BlockSpec((1, d), vec)],
        out_specs=pl.BlockSpec((tm, d), row),
        compiler_params=_cparams(("parallel",)),
        name="post",
    )(x, attn, y, w_glu, b_glu, sn, w_out, w_out, g2)


def _rope_tables(positions):
    inv_freq = ROPE_THETA ** (-jnp.arange(0, ROT_DIM, 2, dtype=F32) / ROT_DIM)
    ang = positions.astype(F32)[:, None] * inv_freq
    cos, sin = jnp.cos(ang), jnp.sin(ang)
    l = positions.shape[0]
    half = ROT_DIM // 2
    ones = jnp.ones((l, HEAD_DIM - ROT_DIM), F32)
    zeros = jnp.zeros((l, HEAD_DIM - ROT_DIM), F32)
    zh = jnp.zeros((l, half), F32)
    cos_t = jnp.concatenate([cos, cos, ones], axis=1)
    s1_t = jnp.concatenate([-sin, zh, zeros], axis=1)
    s2_t = jnp.concatenate([zh, sin, zeros], axis=1)
    tile2 = lambda a: jnp.concatenate([a, a], axis=1)
    return tile2(cos_t), tile2(s1_t), tile2(s2_t)


def kernel(x, c, positions, w_ada, b_ada, ffn1_norm, ffn1_w1, ffn1_w3, ffn1_w2, mix_norm, w_in, q_norm, k_norm, lambda_q1, lambda_k1, lambda_q2, lambda_k2, attn_subln, ssm_a_re, ssm_a_im, ssm_log_dt, ssm_b_re, ssm_b_im, ssm_c_re, ssm_c_im, ssm_d, w_glu, b_glu, ssm_out_norm, w_out, ffn2_norm, ffn2_w1, ffn2_w3, ffn2_w2):
    batch, seq, d = x.shape
    depth = w_ada.shape[0]
    assert batch == 1 and seq % (SSM_T * SSM_CB) == 0
    pos = positions[0]
    cid = pos // CHUNK
    cid_col, cid_row = cid.reshape(seq, 1), cid.reshape(1, seq)
    cos_t, s1_t, s2_t = _rope_tables(pos)
    lane = jnp.arange(V_DIM)
    gsum = (lane[:, None] // HEAD_DIM == lane[None, :] // HEAD_DIM).astype(BF16)
    tile2 = lambda a: jnp.concatenate([a, a]).reshape(1, V_DIM)
    nb = seq // SSM_T
    n_groups = ssm_a_re.shape[1]

    xs = x[0]
    for l in range(depth):
        lam_init = 0.8 - 0.6 * math.exp(-0.3 * l)
        mod = _ada(c, w_ada[l], b_ada[l])
        sh1, sc1, g1, sh2, sc2, g2, sh3, sc3, g3 = jnp.split(mod, 9, axis=-1)
        vec = lambda a: a.reshape(1, -1)

        xs = _ffn(xs, vec(ffn1_norm[l]), sh1, sc1, g1, ffn1_w1[l].astype(BF16),
                  ffn1_w3[l].astype(BF16), ffn1_w2[l].astype(BF16))

        qt, k, vt, u = _proj(xs, vec(mix_norm[l]), sh2, sc2, w_in[l].astype(BF16),
                             tile2(q_norm[l]), tile2(k_norm[l]), cos_t, s1_t, s2_t, gsum)

        lamv = jnp.stack([lambda_q1[l], lambda_k1[l], lambda_q2[l], lambda_k2[l]]).astype(F32)
        attn = _attention(qt, k, vt, cid_col, cid_row, lamv,
                          attn_subln[l].astype(F32).reshape(V_DIM, 1), lam_init)

        at, bm, cm, pr, pi = _ssm_prep(ssm_a_re[l], ssm_a_im[l], ssm_log_dt[l], ssm_b_re[l],
                                       ssm_b_im[l], ssm_c_re[l], ssm_c_im[l], ssm_d[l])
        n_oct, slab = n_groups // SSM_OCT, SSM_OCT * SSM_GROUP
        u_oct = (u.reshape(nb, SSM_T, n_oct, slab).transpose(2, 0, 1, 3)
                 .reshape(n_oct, nb, SSM_T * slab))
        y_oct = _ssm(u_oct, *_ssm_octets(at, bm, cm, pr, pi))
        y = (y_oct.reshape(n_oct, nb, SSM_T, slab).transpose(1, 2, 0, 3)
             .reshape(seq, n_groups * SSM_GROUP))

        xs = _post(xs, attn, y, w_glu[l].astype(BF16), vec(b_glu[l]), vec(ssm_out_norm[l]),
                   w_out[l].astype(BF16), g2)

        xs = _ffn(xs, vec(ffn2_norm[l]), sh3, sc3, g3, ffn2_w1[l].astype(BF16),
                  ffn2_w3[l].astype(BF16), ffn2_w2[l].astype(BF16))
    return xs[None]
```

```python
import functools
import math

import jax
import jax.numpy as jnp
from jax import lax
from jax.experimental import pallas as pl
from jax.experimental.pallas import tpu as pltpu

F32 = jnp.float32
BF16 = jnp.bfloat16

CHUNK = 64
N_HEADS = 8
HEAD_DIM = 64
V_DIM = 128
ROT_DIM = 16
ROPE_THETA = 500000.0
SSM_GROUP = 16
SSM_STATE = 64
EPS = 1e-6
NEG_INF = -1e30
LOG2E = 1.4426950408889634

SSM_T = 16
SSM_W = SSM_T * SSM_GROUP

VMEM_LIMIT = 56 * 1024 * 1024


def _cparams(sem):
    return pltpu.CompilerParams(dimension_semantics=sem, vmem_limit_bytes=VMEM_LIMIT)


def _sigmoid(x):
    return 1.0 / (1.0 + jnp.exp(-x))


def _norm_mod(x, g, shift, scale):
    ms = jnp.mean(x * x, axis=-1, keepdims=True)
    y = x * lax.rsqrt(ms + EPS) * g
    return y * (1.0 + scale) + shift


def _ada_kernel(c_ref, w_ref, b_ref, o_ref):
    c = c_ref[...]
    cond = c * _sigmoid(c)
    o_ref[...] = jnp.dot(cond, w_ref[...], preferred_element_type=F32,
                         precision=lax.Precision.HIGHEST) + b_ref[...]


def _ada(c, w_ada, b_ada, tn=1024):
    d, n = w_ada.shape
    c8 = jnp.broadcast_to(c, (8, d))
    out = pl.pallas_call(
        _ada_kernel,
        out_shape=jax.ShapeDtypeStruct((8, n), F32),
        grid=(n // tn,),
        in_specs=[pl.BlockSpec((8, d), lambda j: (0, 0)),
                  pl.BlockSpec((d, tn), lambda j: (0, j)),
                  pl.BlockSpec((1, tn), lambda j: (0, j))],
        out_specs=pl.BlockSpec((8, tn), lambda j: (0, j)),
        compiler_params=_cparams(("arbitrary",)),
        name="ada",
    )(c8, w_ada, b_ada.reshape(1, n))
    return out[0:1]


def _ffn_kernel(x_ref, g_ref, sh_ref, sc_ref, gate_ref, w1_ref, w3_ref, w2_ref,
                o_ref, h_ref, acc_ref, gu_ref):
    f = pl.program_id(1)
    n_chunks = pl.num_programs(1) - 1

    def gate_up():
        h = h_ref[...]
        a = jnp.dot(h, w1_ref[...], preferred_element_type=F32)
        b = jnp.dot(h, w3_ref[...], preferred_element_type=F32)
        gu_ref[...] = ((a * _sigmoid(a)) * b).astype(BF16)

    def down():
        acc_ref[...] += jnp.dot(gu_ref[...], w2_ref[...], preferred_element_type=F32)

    @pl.when(f == 0)
    def _():
        h = _norm_mod(x_ref[...], g_ref[...], sh_ref[...], sc_ref[...])
        h_ref[...] = h.astype(BF16)
        acc_ref[...] = jnp.zeros_like(acc_ref)
        gate_up()

    @pl.when(jnp.logical_and(f > 0, f < n_chunks))
    def _():
        down()
        gate_up()

    @pl.when(f == n_chunks)
    def _():
        down()
        o_ref[...] = x_ref[...] + (0.5 * gate_ref[...]) * acc_ref[...]


def _ffn(x, g, shift, scale, gate, w1, w3, w2, tm=512, tf=512):
    l, d = x.shape
    n_chunks = w1.shape[1] // tf
    row = lambda i, f: (i, 0)
    vec = lambda i, f: (0, 0)
    up = lambda i, f: (0, jnp.minimum(f, n_chunks - 1))
    return pl.pallas_call(
        _ffn_kernel,
        out_shape=jax.ShapeDtypeStruct((l, d), F32),
        grid=(l // tm, n_chunks + 1),
        in_specs=[pl.BlockSpec((tm, d), row),
                  pl.BlockSpec((1, d), vec), pl.BlockSpec((1, d), vec),
                  pl.BlockSpec((1, d), vec), pl.BlockSpec((1, d), vec),
                  pl.BlockSpec((d, tf), up), pl.BlockSpec((d, tf), up),
                  pl.BlockSpec((tf, d), lambda i, f: (jnp.maximum(f - 1, 0), 0))],
        out_specs=pl.BlockSpec((tm, d), row),
        scratch_shapes=[pltpu.VMEM((tm, d), BF16), pltpu.VMEM((tm, d), F32),
                        pltpu.VMEM((tm, tf), BF16)],
        compiler_params=_cparams(("parallel", "arbitrary")),
        name="ffn",
    )(x, g, shift, scale, gate, w1, w3, w2)


def _proj_kernel(x_ref, g_ref, sh_ref, sc_ref, w_ref, qn_ref, kn_ref, cos_ref, s1_ref,
                 s2_ref, gsum_ref, qt_ref, k_ref, vt_ref, u_ref, h_ref, *, q_scale):
    j = pl.program_id(1)

    @pl.when(j == 0)
    def _():
        h = _norm_mod(x_ref[...], g_ref[...], sh_ref[...], sc_ref[...])
        h_ref[...] = h.astype(BF16)

    p = jnp.dot(h_ref[...], w_ref[...], preferred_element_type=F32)

    def qk_rot(gain_ref, scale, hd):
        pc = p[:, hd * V_DIM:(hd + 1) * V_DIM]
        ss = jnp.dot((pc * pc).astype(BF16), gsum_ref[...], preferred_element_type=F32)
        y = pc * lax.rsqrt(ss * (1.0 / HEAD_DIM) + EPS) * (gain_ref[...] * scale)
        return (y * cos_ref[...] + pltpu.roll(y, V_DIM - ROT_DIM // 2, 1) * s1_ref[...]
                + pltpu.roll(y, ROT_DIM // 2, 1) * s2_ref[...])

    @pl.when(j == 0)
    def _():
        for hd in range(N_HEADS):
            qt_ref[hd, 0] = qk_rot(qn_ref, q_scale, hd).T.astype(BF16)

    @pl.when(j == 1)
    def _():
        for hd in range(N_HEADS):
            k_ref[hd] = qk_rot(kn_ref, 1.0, hd).astype(BF16)

    @pl.when(j == 2)
    def _():
        for hd in range(N_HEADS):
            vt_ref[hd, 0] = p[:, hd * V_DIM:(hd + 1) * V_DIM].T.astype(BF16)

    @pl.when(j == 3)
    def _():
        u_ref[...] = p.astype(BF16)


def _proj(x, g, shift, scale, w_in, qn, kn, cos_t, s1_t, s2_t, gsum, tm=512):
    l, d = x.shape
    tn = 1024
    row = lambda i, j: (i, 0)
    vec = lambda i, j: (0, 0)
    hm = lambda i, j: (0, i, 0)
    tm_t = lambda i, j: (0, i, 0, 0)
    head_major = jax.ShapeDtypeStruct((N_HEADS, l, V_DIM), BF16)
    tiled_t = jax.ShapeDtypeStruct((N_HEADS, l // tm, V_DIM, tm), BF16)
    return pl.pallas_call(
        functools.partial(_proj_kernel, q_scale=HEAD_DIM ** -0.5 * LOG2E),
        out_shape=(tiled_t, head_major, tiled_t, jax.ShapeDtypeStruct((l, tn), BF16)),
        grid=(l // tm, 4),
        in_specs=[pl.BlockSpec((tm, d), row),
                  pl.BlockSpec((1, d), vec), pl.BlockSpec((1, d), vec), pl.BlockSpec((1, d), vec),
                  pl.BlockSpec((d, tn), lambda i, j: (0, j)),
                  pl.BlockSpec((1, V_DIM), vec), pl.BlockSpec((1, V_DIM), vec),
                  pl.BlockSpec((tm, V_DIM), row), pl.BlockSpec((tm, V_DIM), row),
                  pl.BlockSpec((tm, V_DIM), row),
                  pl.BlockSpec((V_DIM, V_DIM), vec)],
        out_specs=(pl.BlockSpec((N_HEADS, 1, V_DIM, tm), tm_t), pl.BlockSpec((N_HEADS, tm, V_DIM), hm),
                   pl.BlockSpec((N_HEADS, 1, V_DIM, tm), tm_t), pl.BlockSpec((tm, tn), row)),
        scratch_shapes=[pltpu.VMEM((tm, d), BF16)],
        compiler_params=_cparams(("parallel", "arbitrary")),
        name="proj",
    )(x, g, shift, scale, w_in, qn, kn, cos_t, s1_t, s2_t, gsum)


ATTN_CW = 256
ATTN_BOUND_SLACK = 1.02
ATTN_BOUND_MAX = 48.0


def _tree8(x, op):
    while x.shape[0] > 8:
        h = x.shape[0] // 2
        x = op(x[:h], x[h:])
    return x


def _attn_kernel(qt_ref, k_ref, vt_ref, cq_ref, ck_ref, lamv_ref, subg_ref, o_ref,
                 q2_ref, m_ref, l_ref, acc_ref, s_buf, p_buf, a_buf, kn_ref, b_ref, l8_ref,
                 *, tq, lam_init):
    i = pl.program_id(1)
    n_sub = 2 * tq // ATTN_CW
    n_tiles = k_ref.shape[1] // tq

    @pl.when(i == 0)
    def _():
        ones = jnp.ones((V_DIM, V_DIM), BF16)

        def chunk(r, mx):
            kk = k_ref[0, pl.ds(pl.multiple_of(r * tq, tq), tq), :].astype(F32)
            n2 = jnp.dot((kk * kk).astype(BF16), ones, preferred_element_type=F32)
            return jnp.maximum(mx, jnp.max(_tree8(n2, jnp.maximum), axis=0, keepdims=True))

        kn_ref[...] = lax.fori_loop(0, n_tiles, chunk, jnp.zeros(kn_ref.shape, F32))

    qt = qt_ref[0, 0]
    feat = lax.broadcasted_iota(jnp.int32, qt.shape, 0)
    zero = jnp.zeros_like(qt)
    q2_ref[:, :tq] = jnp.where(feat < HEAD_DIM, qt, zero)
    q2_ref[:, tq:] = jnp.where(feat >= HEAD_DIM, qt, zero)
    acc_ref[...] = jnp.zeros_like(acc_ref)
    p_buf[1] = jnp.zeros(p_buf.shape[1:], BF16)

    q2f = q2_ref[...].astype(F32)
    qn2 = jnp.sum(_tree8(q2f * q2f, jnp.add), axis=0, keepdims=True)
    kn2 = jnp.concatenate([kn_ref[...]] * (2 * tq // V_DIM), axis=1)
    bound = jnp.sqrt(qn2 * kn2) * ATTN_BOUND_SLACK
    b_ref[...] = bound
    bounded_ok = jnp.max(bound) <= ATTN_BOUND_MAX

    def cols(c):
        return slice(c * ATTN_CW, (c + 1) * ATTN_CW)

    def key_tile(j):
        return k_ref[0, pl.ds(pl.multiple_of(j * tq, tq), tq), :]

    def mask(c, s):
        q0 = (c * ATTN_CW) % tq
        return jnp.where(ck_ref[...] <= cq_ref[:, q0:q0 + ATTN_CW], s, NEG_INF)

    def pipeline(tile, last_pv, pairs):
        if pairs:
            odd = i % 2

            @pl.when(odd == 1)
            def _():
                tile(0, False)

            def body(jj, carry):
                tile(odd + 2 * jj, False)
                tile(odd + 2 * jj + 1, False)
                return carry

            lax.fori_loop(0, i // 2, body, 0)
        else:
            def body(j, carry):
                tile(j, False)
                return carry

            lax.fori_loop(0, i, body, 0)
        tile(i, True)
        last_pv(i, n_sub - 1)

    def qk(j, c):
        s_buf[c % 2] = jnp.dot(key_tile(j), q2_ref[:, cols(c)], preferred_element_type=F32)

    def stages(j, masked, pv_fn, softmax_fn):
        for c in range(n_sub):
            if c + 1 < n_sub:
                qk(j, c + 1)
            elif not masked:
                qk(j + 1, 0)
            if c > 0:
                pv_fn(j, c - 1)
            else:
                pv_fn(jnp.maximum(j - 1, 0), n_sub - 1)
            softmax_fn(c, masked)

    def b_softmax(c, masked):
        s = s_buf[c % 2]
        if masked:
            s = mask(c, s)
        p = jnp.exp2(s - b_ref[:, cols(c)])
        l8_ref[:, cols(c)] += _tree8(p, jnp.add)
        p_buf[c % 2] = p.astype(BF16)

    def b_pv(j, c):
        acc_ref[:, cols(c)] += jnp.dot(vt_ref[0, j], p_buf[c % 2], preferred_element_type=F32)

    @pl.when(bounded_ok)
    def _():
        l8_ref[...] = jnp.zeros_like(l8_ref)
        qk(0, 0)
        pipeline(functools.partial(stages, pv_fn=b_pv, softmax_fn=b_softmax), b_pv, pairs=True)
        l_ref[...] = jnp.sum(l8_ref[...], axis=0, keepdims=True)

    def pv(j, c):
        acc_ref[:, cols(c)] = (a_buf[c % 2] * acc_ref[:, cols(c)]
                               + jnp.dot(vt_ref[0, j], p_buf[c % 2],
                                         preferred_element_type=F32))

    def softmax(c, masked):
        s = s_buf[c % 2]
        if masked:
            s = mask(c, s)
        m_prev = m_ref[:, cols(c)]
        m_new = jnp.maximum(m_prev, jnp.max(_tree8(s, jnp.maximum), axis=0, keepdims=True))
        alpha = jnp.exp2(m_prev - m_new)
        p = jnp.exp2(s - m_new)
        l_ref[:, cols(c)] = alpha * l_ref[:, cols(c)] + jnp.sum(_tree8(p, jnp.add), axis=0,
                                                                keepdims=True)
        m_ref[:, cols(c)] = m_new
        a_buf[c % 2] = alpha
        p_buf[c % 2] = p.astype(BF16)

    @pl.when(jnp.logical_not(bounded_ok))
    def _():
        m_ref[...] = jnp.full(m_ref.shape, NEG_INF, F32)
        l_ref[...] = jnp.zeros_like(l_ref)
        a_buf[1] = jnp.zeros(a_buf.shape[1:], F32)
        qk(0, 0)
        pipeline(functools.partial(stages, pv_fn=pv, softmax_fn=softmax), pv, pairs=False)

    o = acc_ref[...] / l_ref[...]
    lv = lamv_ref[...]
    lam = (jnp.exp(jnp.sum(lv[0:1] * lv[1:2], axis=1, keepdims=True))
           - jnp.exp(jnp.sum(lv[2:3] * lv[3:4], axis=1, keepdims=True)) + lam_init)
    od = o[:, :tq] - lam * o[:, tq:]
    ms = jnp.mean(od * od, axis=0, keepdims=True)
    on = od * lax.rsqrt(ms + EPS) * (subg_ref[...] * (1.0 - lam_init))
    o_ref[...] = on.T.astype(BF16)


def _attention(qt, k, vt, cid_col, cid_row, lamv, subg_col, lam_init):
    _, nt, _, tq = qt.shape
    l = nt * tq
    return pl.pallas_call(
        functools.partial(_attn_kernel, tq=tq, lam_init=lam_init),
        out_shape=jax.ShapeDtypeStruct((l, N_HEADS * V_DIM), BF16),
        grid=(N_HEADS, nt),
        in_specs=[pl.BlockSpec((1, 1, V_DIM, tq), lambda h, i: (h, i, 0, 0)),
                  pl.BlockSpec((1, l, V_DIM), lambda h, i: (h, 0, 0)),
                  pl.BlockSpec((1, nt, V_DIM, tq), lambda h, i: (h, 0, 0, 0)),
                  pl.BlockSpec((1, tq), lambda h, i: (0, i)),
                  pl.BlockSpec((tq, 1), lambda h, i: (i, 0)),
                  pl.BlockSpec((4, HEAD_DIM), lambda h, i: (0, 0)),
                  pl.BlockSpec((V_DIM, 1), lambda h, i: (0, 0))],
        out_specs=pl.BlockSpec((tq, V_DIM), lambda h, i: (i, h)),
        scratch_shapes=[pltpu.VMEM((V_DIM, 2 * tq), BF16), pltpu.VMEM((1, 2 * tq), F32),
                        pltpu.VMEM((1, 2 * tq), F32), pltpu.VMEM((V_DIM, 2 * tq), F32),
                        pltpu.VMEM((2, tq, ATTN_CW), F32), pltpu.VMEM((2, tq, ATTN_CW), BF16),
                        pltpu.VMEM((2, 1, ATTN_CW), F32), pltpu.VMEM((1, V_DIM), F32),
                        pltpu.VMEM((1, 2 * tq), F32), pltpu.VMEM((8, 2 * tq), F32)],
        compiler_params=_cparams(("parallel", "arbitrary")),
        name="attn",
    )(qt, k, vt, cid_row, cid_col, lamv, subg_col)


SSM_OCT = 8
SSM_CB = 256


def _ssm_kernel(u_ref, kb_ref, bm_ref, cm_ref, pr_ref, pi_ref, y_ref, carry_ref, at_ref):
    slab = kb_ref.shape[2]

    @pl.when(pl.program_id(1) == 0)
    def _():
        carry_ref[...] = jnp.zeros_like(carry_ref)
        at_ref[...] = jnp.zeros_like(at_ref)
        for sg in range(SSM_T):
            for tau in range(sg, SSM_T):
                at_ref[sg * slab:(sg + 1) * slab, tau * slab:(tau + 1) * slab] = kb_ref[0, tau - sg]

    u = u_ref[0]
    cb = u.shape[0]
    half = carry_ref.shape[1] // 2
    pr, pi = pr_ref[0], pi_ref[0]

    def cmul(lvl, z):
        return pr[lvl:lvl + 1] * z + pi[lvl:lvl + 1] * pltpu.roll(z, half, 1)

    x = jnp.dot(u, bm_ref[0], preferred_element_type=F32)
    rows = lax.broadcasted_iota(jnp.int32, x.shape, 0)
    carry = carry_ref[...]
    x = x + jnp.where(rows == 0, cmul(0, carry), 0.0)

    def shift_down(z, d):
        if d % 8 == 0:
            return jnp.concatenate([jnp.zeros((d, z.shape[1]), F32), z[:cb - d]], axis=0)
        return jnp.where(rows >= d, pltpu.roll(z, d, 0), 0.0)

    for lvl in range(cb.bit_length() - 1):
        x = x + cmul(lvl, shift_down(x, 1 << lvl))
    carry_ref[...] = x[cb - 1:cb]
    xprev = shift_down(x, 1) + jnp.where(rows == 0, carry, 0.0)
    xb = xprev.astype(BF16)
    wn = 2 * slab
    for n in range(u.shape[1] // wn):
        kmax = (n + 1) * wn
        y = jnp.dot(u[:, :kmax], at_ref[:kmax, n * wn:(n + 1) * wn], preferred_element_type=F32)
        y = y + jnp.dot(xb, cm_ref[0, :, n * wn:(n + 1) * wn], preferred_element_type=F32)
        for h in range(wn // slab):
            tau = n * (wn // slab) + h
            y_ref[0, pl.ds(tau, cb, stride=SSM_T), :] = y[:, h * slab:(h + 1) * slab]


def _ssm(u_oct, at, bm, cm, pr, pi):
    n_oct, nb, w = u_oct.shape
    ns = bm.shape[2]
    per_oct = lambda a: pl.BlockSpec((1,) + a.shape[1:], lambda o, t: (o, 0, 0))
    return pl.pallas_call(
        _ssm_kernel,
        out_shape=jax.ShapeDtypeStruct((n_oct, nb * SSM_T, w // SSM_T), F32),
        grid=(n_oct, nb // SSM_CB),
        in_specs=[pl.BlockSpec((1, SSM_CB, w), lambda o, t: (o, t, 0)),
                  pl.BlockSpec((1,) + at.shape[1:], lambda o, t: (o, 0, 0, 0)),
                  per_oct(bm), per_oct(cm), per_oct(pr), per_oct(pi)],
        out_specs=pl.BlockSpec((1, SSM_CB * SSM_T, w // SSM_T), lambda o, t: (o, t, 0)),
        scratch_shapes=[pltpu.VMEM((1, ns), F32), pltpu.VMEM((w, w), BF16)],
        compiler_params=_cparams(("parallel", "arbitrary")),
        name="ssm",
    )(u_oct, at, bm, cm, pr, pi)


def _ssm_octets(kj, bm, cm, pr, pi):
    g = kj.shape[0]
    no, t, c, p = g // SSM_OCT, SSM_T, SSM_GROUP, SSM_STATE
    eye = jnp.eye(SSM_OCT, dtype=BF16)
    kj, bm, cm = kj.astype(BF16), bm.astype(BF16), cm.astype(BF16)
    at_o = jnp.einsum('Ggjcd,gk->Gjgdkc', kj.reshape(no, SSM_OCT, t, c, c), eye)
    at_o = at_o.reshape(no, t, SSM_OCT * c, SSM_OCT * c)
    bm_o = jnp.einsum('Ggsdrp,gk->Gsgdrkp', bm.reshape(no, SSM_OCT, t, c, 2, p), eye)
    bm_o = bm_o.reshape(no, t * SSM_OCT * c, 2 * SSM_OCT * p)
    cm_o = jnp.einsum('Ggrptc,gk->Grgptkc', cm.reshape(no, SSM_OCT, 2, p, t, c), eye)
    cm_o = cm_o.reshape(no, 2 * SSM_OCT * p, t * SSM_OCT * c)
    lanes = lambda a: (a.reshape(no, SSM_OCT, 16, 2, p).transpose(0, 2, 3, 1, 4)
                       .reshape(no, 16, 2 * SSM_OCT * p))
    return at_o, bm_o, cm_o, lanes(pr), lanes(pi)


def _ssm_prep(a_re, a_im, log_dt, b_re, b_im, c_re, c_im, d_skip):
    hp = lax.Precision.HIGHEST
    lr = jnp.minimum(a_re, -1e-4)
    li = a_im
    dt = jnp.exp(log_dt)[:, None]

    def lpow(n):
        n = jnp.asarray(n, F32)[..., None, None]
        mag = jnp.exp(n * (lr * dt))
        ang = n * (li * dt)
        return mag * jnp.cos(ang), mag * jnp.sin(ang)

    th = li * dt
    nr = jnp.expm1(lr * dt) * jnp.cos(th) - 2.0 * jnp.sin(0.5 * th) ** 2
    ni = jnp.exp(lr * dt) * jnp.sin(th)
    den = lr * lr + li * li
    fr = (nr * lr + ni * li) / den
    fi = (ni * lr - nr * li) / den
    bbr = fr[..., None] * b_re - fi[..., None] * b_im
    bbi = fr[..., None] * b_im + fi[..., None] * b_re

    t = jnp.arange(SSM_T)
    pjr, pji = lpow(t)
    mr = pjr[..., None] * bbr[None] - pji[..., None] * bbi[None]
    mi = pjr[..., None] * bbi[None] + pji[..., None] * bbr[None]
    kj = (jnp.einsum('gcp,jgpd->gjcd', c_re, mr, precision=hp)
          - jnp.einsum('gcp,jgpd->gjcd', c_im, mi, precision=hp))
    eye_c = jnp.eye(SSM_GROUP, dtype=F32)
    kj = kj.at[:, 0].add(eye_c[None] * d_skip[:, :, None])
    g = a_re.shape[0]

    rr, ri = lpow(SSM_T - 1 - t)
    sr = rr[..., None] * bbr[None] - ri[..., None] * bbi[None]
    si = rr[..., None] * bbi[None] + ri[..., None] * bbr[None]
    bm = jnp.concatenate([sr, si], axis=2)
    bm = bm.transpose(1, 0, 3, 2).reshape(g, SSM_W, 2 * SSM_STATE)

    qr, qi = lpow(t + 1)
    wr = c_re[None] * qr[:, :, None, :] - c_im[None] * qi[:, :, None, :]
    wi = c_re[None] * qi[:, :, None, :] + c_im[None] * qr[:, :, None, :]
    cm = jnp.concatenate([wr, -wi], axis=3)
    cm = cm.transpose(1, 3, 0, 2).reshape(g, 2 * SSM_STATE, SSM_W)

    er, ei = lpow(SSM_T * (2 ** jnp.arange(16)))
    pr = jnp.concatenate([er, er], axis=2).transpose(1, 0, 2)
    pi = jnp.concatenate([-ei, ei], axis=2).transpose(1, 0, 2)
    return kj, bm, cm, pr, pi


def _post_kernel(x_ref, attn_ref, y_ref, wglu_ref, bglu_ref, sn_ref, woa_ref, wob_ref,
                 g2_ref, o_ref):
    y = jnp.concatenate([y_ref[s] for s in range(y_ref.shape[0])], axis=1)
    y = jax.nn.gelu(y, approximate=True)
    z = jnp.dot(y.astype(BF16), wglu_ref[...], preferred_element_type=F32) + bglu_ref[...]
    y = y * _sigmoid(z)
    ms = jnp.mean(y * y, axis=-1, keepdims=True)
    y = y * lax.rsqrt(ms + EPS) * sn_ref[...]
    mixed = (jnp.dot(attn_ref[...], woa_ref[...], preferred_element_type=F32)
             + jnp.dot(y.astype(BF16), wob_ref[...], preferred_element_type=F32))
    o_ref[...] = x_ref[...] + g2_ref[...] * mixed


def _post(x, attn, y, w_glu, b_glu, sn, w_out, g2, tm=256):
    l, d = x.shape
    n_slab, _, slab = y.shape
    w = n_slab * slab
    row = lambda i: (i, 0)
    vec = lambda i: (0, 0)
    return pl.pallas_call(
        _post_kernel,
        out_shape=jax.ShapeDtypeStruct((l, d), F32),
        grid=(l // tm,),
        in_specs=[pl.BlockSpec((tm, d), row), pl.BlockSpec((tm, w), row),
                  pl.BlockSpec((n_slab, tm, slab), lambda i: (0, i, 0)),
                  pl.BlockSpec((w, w), vec), pl.BlockSpec((1, w), vec), pl.BlockSpec((1, w), vec),
                  pl.BlockSpec((w, d), lambda i: (0, 0)), pl.BlockSpec((w, d), lambda i: (1, 0)),
                  pl.BlockSpec((1, d), vec)],
        out_specs=pl.BlockSpec((tm, d), row),
        compiler_params=_cparams(("parallel",)),
        name="post",
    )(x, attn, y, w_glu, b_glu, sn, w_out, w_out, g2)


def _rope_tables(positions):
    inv_freq = ROPE_THETA ** (-jnp.arange(0, ROT_DIM, 2, dtype=F32) / ROT_DIM)
    half = ROT_DIM // 2
    dim = jnp.arange(V_DIM) % HEAD_DIM
    ang = positions.astype(F32)[:, None] * inv_freq[dim % half][None, :]
    cos, sin = jnp.cos(ang), jnp.sin(ang)
    first, second = dim < half, (dim >= half) & (dim < ROT_DIM)
    cos_t = jnp.where(first | second, cos, 1.0)
    s1_t = jnp.where(first, -sin, 0.0)
    s2_t = jnp.where(second, sin, 0.0)
    return cos_t, s1_t, s2_t


def kernel(x, c, positions, w_ada, b_ada, ffn1_norm, ffn1_w1, ffn1_w3, ffn1_w2, mix_norm, w_in, q_norm, k_norm, lambda_q1, lambda_k1, lambda_q2, lambda_k2, attn_subln, ssm_a_re, ssm_a_im, ssm_log_dt, ssm_b_re, ssm_b_im, ssm_c_re, ssm_c_im, ssm_d, w_glu, b_glu, ssm_out_norm, w_out, ffn2_norm, ffn2_w1, ffn2_w3, ffn2_w2):
    batch, seq, d = x.shape
    depth = w_ada.shape[0]
    assert batch == 1 and seq % (SSM_T * SSM_CB) == 0
    pos = positions[0]
    cid = pos // CHUNK
    cid_col, cid_row = cid.reshape(seq, 1), cid.reshape(1, seq)
    cos_t, s1_t, s2_t = _rope_tables(pos)
    lane = jnp.arange(V_DIM)
    gsum = (lane[:, None] // HEAD_DIM == lane[None, :] // HEAD_DIM).astype(BF16)
    tile2 = lambda a: jnp.concatenate([a, a]).reshape(1, V_DIM)
    nb = seq // SSM_T
    n_groups = ssm_a_re.shape[1]

    xs = x[0]
    for l in range(depth):
        lam_init = 0.8 - 0.6 * math.exp(-0.3 * l)
        mod = _ada(c, w_ada[l], b_ada[l])
        sh1, sc1, g1, sh2, sc2, g2, sh3, sc3, g3 = jnp.split(mod, 9, axis=-1)
        vec = lambda a: a.reshape(1, -1)

        xs = _ffn(xs, vec(ffn1_norm[l]), sh1, sc1, g1, ffn1_w1[l].astype(BF16),
                  ffn1_w3[l].astype(BF16), ffn1_w2[l].astype(BF16))

        qt, k, vt, u = _proj(xs, vec(mix_norm[l]), sh2, sc2, w_in[l].astype(BF16),
                             tile2(q_norm[l]), tile2(k_norm[l]), cos_t, s1_t, s2_t, gsum)

        lamv = jnp.stack([lambda_q1[l], lambda_k1[l], lambda_q2[l], lambda_k2[l]]).astype(F32)
        attn = _attention(qt, k, vt, cid_col, cid_row, lamv,
                          attn_subln[l].astype(F32).reshape(V_DIM, 1), lam_init)

        at, bm, cm, pr, pi = _ssm_prep(ssm_a_re[l], ssm_a_im[l], ssm_log_dt[l], ssm_b_re[l],
                                       ssm_b_im[l], ssm_c_re[l], ssm_c_im[l], ssm_d[l])
        n_oct, slab = n_groups // SSM_OCT, SSM_OCT * SSM_GROUP
        u_oct = (u.reshape(nb, SSM_T, n_oct, slab).transpose(2, 0, 1, 3)
                 .reshape(n_oct, nb, SSM_T * slab))
        y = _ssm(u_oct, *_ssm_octets(at, bm, cm, pr, pi))

        xs = _post(xs, attn, y, w_glu[l].astype(BF16), vec(b_glu[l]), vec(ssm_out_norm[l]),
                   w_out[l].astype(BF16), g2)

        xs = _ffn(xs, vec(ffn2_norm[l]), sh3, sc3, g3, ffn2_w1[l].astype(BF16),
                  ffn2_w3[l].astype(BF16), ffn2_w2[l].astype(BF16))
    return xs[None]
```

```python
import functools
import math

import jax
import jax.numpy as jnp
from jax import lax
from jax.experimental import pallas as pl
from jax.experimental.pallas import tpu as pltpu

F32 = jnp.float32
BF16 = jnp.bfloat16

CHUNK = 64
N_HEADS = 8
HEAD_DIM = 64
V_DIM = 128
ROT_DIM = 16
ROPE_THETA = 500000.0
SSM_GROUP = 16
SSM_STATE = 64
EPS = 1e-6
NEG_INF = -1e30
LOG2E = 1.4426950408889634

SSM_T = 16
SSM_W = SSM_T * SSM_GROUP

VMEM_LIMIT = 56 * 1024 * 1024


def _cparams(sem):
    return pltpu.CompilerParams(dimension_semantics=sem, vmem_limit_bytes=VMEM_LIMIT)


def _sigmoid(x):
    return 1.0 / (1.0 + jnp.exp(-x))


def _norm_mod(x, g, shift, scale):
    ms = jnp.mean(x * x, axis=-1, keepdims=True)
    y = x * lax.rsqrt(ms + EPS) * g
    return y * (1.0 + scale) + shift


def _ada_kernel(c_ref, w_ref, b_ref, o_ref):
    c = c_ref[...]
    cond = c * _sigmoid(c)
    o_ref[...] = jnp.dot(cond, w_ref[...], preferred_element_type=F32,
                         precision=lax.Precision.HIGHEST) + b_ref[...]


def _ada(c, w_ada, b_ada, tn=1024):
    d, n = w_ada.shape
    c8 = jnp.broadcast_to(c, (8, d))
    out = pl.pallas_call(
        _ada_kernel,
        out_shape=jax.ShapeDtypeStruct((8, n), F32),
        grid=(n // tn,),
        in_specs=[pl.BlockSpec((8, d), lambda j: (0, 0)),
                  pl.BlockSpec((d, tn), lambda j: (0, j)),
                  pl.BlockSpec((1, tn), lambda j: (0, j))],
        out_specs=pl.BlockSpec((8, tn), lambda j: (0, j)),
        compiler_params=_cparams(("arbitrary",)),
        name="ada",
    )(c8, w_ada, b_ada.reshape(1, n))
    return out[0:1]


def _ffn_chunk_kernel(x_ref, g_ref, sh_ref, sc_ref, gate_ref, w1_ref, w3_ref, w2_ref,
                      o_ref, h_ref, acc_ref):
    f = pl.program_id(1)

    @pl.when(f == 0)
    def _():
        h = _norm_mod(x_ref[...], g_ref[...], sh_ref[...], sc_ref[...])
        h_ref[...] = h.astype(BF16)
        acc_ref[...] = jnp.zeros_like(acc_ref)

    h = h_ref[...]
    a = jnp.dot(h, w1_ref[0], preferred_element_type=F32)
    b = jnp.dot(h, w3_ref[0], preferred_element_type=F32)
    g = (a * _sigmoid(a)) * b
    acc_ref[...] += jnp.dot(g.astype(BF16), w2_ref[...], preferred_element_type=F32)

    @pl.when(f == pl.num_programs(1) - 1)
    def _():
        o_ref[...] = x_ref[...] + (0.5 * gate_ref[...]) * acc_ref[...]


def _ffn_kernel(x_ref, g_ref, sh_ref, sc_ref, gate_ref, w1_ref, w3_ref, w2_ref,
                o_ref, h_ref, acc_ref, gu_ref):
    f = pl.program_id(1)
    n_chunks = pl.num_programs(1) - 1

    def gate_up():
        h = h_ref[...]
        a = jnp.dot(h, w1_ref[0], preferred_element_type=F32)
        b = jnp.dot(h, w3_ref[0], preferred_element_type=F32)
        gu_ref[...] = ((a * _sigmoid(a)) * b).astype(BF16)

    def down():
        acc_ref[...] += jnp.dot(gu_ref[...], w2_ref[...], preferred_element_type=F32)

    @pl.when(f == 0)
    def _():
        h = _norm_mod(x_ref[...], g_ref[...], sh_ref[...], sc_ref[...])
        h_ref[...] = h.astype(BF16)
        acc_ref[...] = jnp.zeros_like(acc_ref)
        gate_up()

    @pl.when(jnp.logical_and(f > 0, f < n_chunks))
    def _():
        down()
        gate_up()

    @pl.when(f == n_chunks)
    def _():
        down()
        o_ref[...] = x_ref[...] + (0.5 * gate_ref[...]) * acc_ref[...]


def _chunked(w, tf):
    d, dff = w.shape
    return w.astype(BF16).reshape(d, dff // tf, tf).transpose(1, 0, 2)


def _ffn(x, g, shift, scale, gate, w1, w3, w2, *, skewed, tm=512, tf=512):
    l, d = x.shape
    n_chunks = w1.shape[1] // tf
    w1, w3, w2 = _chunked(w1, tf), _chunked(w3, tf), w2.astype(BF16)
    row = lambda i, f: (i, 0)
    vec = lambda i, f: (0, 0)
    if skewed:
        body, steps = _ffn_kernel, n_chunks + 1
        up = lambda i, f: (jnp.minimum(f, n_chunks - 1), 0, 0)
        dn = lambda i, f: (jnp.maximum(f - 1, 0), 0)
        scratch = [pltpu.VMEM((tm, d), BF16), pltpu.VMEM((tm, d), F32), pltpu.VMEM((tm, tf), BF16)]
    else:
        body, steps = _ffn_chunk_kernel, n_chunks
        up = lambda i, f: (f, 0, 0)
        dn = lambda i, f: (f, 0)
        scratch = [pltpu.VMEM((tm, d), BF16), pltpu.VMEM((tm, d), F32)]
    return pl.pallas_call(
        body,
        out_shape=jax.ShapeDtypeStruct((l, d), F32),
        grid=(l // tm, steps),
        in_specs=[pl.BlockSpec((tm, d), row),
                  pl.BlockSpec((1, d), vec), pl.BlockSpec((1, d), vec),
                  pl.BlockSpec((1, d), vec), pl.BlockSpec((1, d), vec),
                  pl.BlockSpec((1, d, tf), up), pl.BlockSpec((1, d, tf), up),
                  pl.BlockSpec((tf, d), dn)],
        out_specs=pl.BlockSpec((tm, d), row),
        scratch_shapes=scratch,
        compiler_params=_cparams(("parallel", "arbitrary")),
        name="ffn_skewed" if skewed else "ffn",
    )(x, g, shift, scale, gate, w1, w3, w2)


def _proj_kernel(x_ref, g_ref, sh_ref, sc_ref, w_ref, qn_ref, kn_ref, cos_ref, s1_ref,
                 s2_ref, gsum_ref, qt_ref, k_ref, vt_ref, u_ref, h_ref, *, q_scale):
    j = pl.program_id(1)

    @pl.when(j == 0)
    def _():
        h = _norm_mod(x_ref[...], g_ref[...], sh_ref[...], sc_ref[...])
        h_ref[...] = h.astype(BF16)

    p = jnp.dot(h_ref[...], w_ref[...], preferred_element_type=F32)

    def qk_rot(gain_ref, scale, hd):
        pc = p[:, hd * V_DIM:(hd + 1) * V_DIM]
        ss = jnp.dot((pc * pc).astype(BF16), gsum_ref[...], preferred_element_type=F32)
        y = pc * lax.rsqrt(ss * (1.0 / HEAD_DIM) + EPS) * (gain_ref[...] * scale)
        return (y * cos_ref[...] + pltpu.roll(y, V_DIM - ROT_DIM // 2, 1) * s1_ref[...]
                + pltpu.roll(y, ROT_DIM // 2, 1) * s2_ref[...])

    @pl.when(j == 0)
    def _():
        for hd in range(N_HEADS):
            qt_ref[hd, 0] = qk_rot(qn_ref, q_scale, hd).T.astype(BF16)

    @pl.when(j == 1)
    def _():
        for hd in range(N_HEADS):
            k_ref[hd] = qk_rot(kn_ref, 1.0, hd).astype(BF16)

    @pl.when(j == 2)
    def _():
        for hd in range(N_HEADS):
            vt_ref[hd, 0] = p[:, hd * V_DIM:(hd + 1) * V_DIM].T.astype(BF16)

    @pl.when(j == 3)
    def _():
        u_ref[...] = p.astype(BF16)


def _proj(x, g, shift, scale, w_in, qn, kn, cos_t, s1_t, s2_t, gsum, tm=512):
    l, d = x.shape
    tn = 1024
    row = lambda i, j: (i, 0)
    vec = lambda i, j: (0, 0)
    hm = lambda i, j: (0, i, 0)
    tm_t = lambda i, j: (0, i, 0, 0)
    head_major = jax.ShapeDtypeStruct((N_HEADS, l, V_DIM), BF16)
    tiled_t = jax.ShapeDtypeStruct((N_HEADS, l // tm, V_DIM, tm), BF16)
    return pl.pallas_call(
        functools.partial(_proj_kernel, q_scale=HEAD_DIM ** -0.5 * LOG2E),
        out_shape=(tiled_t, head_major, tiled_t, jax.ShapeDtypeStruct((l, tn), BF16)),
        grid=(l // tm, 4),
        in_specs=[pl.BlockSpec((tm, d), row),
                  pl.BlockSpec((1, d), vec), pl.BlockSpec((1, d), vec), pl.BlockSpec((1, d), vec),
                  pl.BlockSpec((d, tn), lambda i, j: (0, j)),
                  pl.BlockSpec((1, V_DIM), vec), pl.BlockSpec((1, V_DIM), vec),
                  pl.BlockSpec((tm, V_DIM), row), pl.BlockSpec((tm, V_DIM), row),
                  pl.BlockSpec((tm, V_DIM), row),
                  pl.BlockSpec((V_DIM, V_DIM), vec)],
        out_specs=(pl.BlockSpec((N_HEADS, 1, V_DIM, tm), tm_t), pl.BlockSpec((N_HEADS, tm, V_DIM), hm),
                   pl.BlockSpec((N_HEADS, 1, V_DIM, tm), tm_t), pl.BlockSpec((tm, tn), row)),
        scratch_shapes=[pltpu.VMEM((tm, d), BF16)],
        compiler_params=_cparams(("parallel", "arbitrary")),
        name="proj",
    )(x, g, shift, scale, w_in, qn, kn, cos_t, s1_t, s2_t, gsum)


ATTN_CW = 256
ATTN_BOUND_SLACK = 1.02
ATTN_BOUND_MAX = 48.0


def _tree8(x, op):
    while x.shape[0] > 8:
        h = x.shape[0] // 2
        x = op(x[:h], x[h:])
    return x


def _attn_kernel(qt_ref, k_ref, vt_ref, cq_ref, ck_ref, lamv_ref, subg_ref, o_ref,
                 q2_ref, m_ref, l_ref, acc_ref, s_buf, p_buf, a_buf, kn_ref, b_ref, l8_ref,
                 *, tq, lam_init):
    i = pl.program_id(1)
    n_sub = 2 * tq // ATTN_CW
    n_tiles = k_ref.shape[1] // tq

    @pl.when(i == 0)
    def _():
        ones = jnp.ones((V_DIM, V_DIM), BF16)

        def chunk(r, mx):
            kk = k_ref[0, pl.ds(pl.multiple_of(r * tq, tq), tq), :].astype(F32)
            n2 = jnp.dot((kk * kk).astype(BF16), ones, preferred_element_type=F32)
            return jnp.maximum(mx, jnp.max(_tree8(n2, jnp.maximum), axis=0, keepdims=True))

        kn_ref[...] = lax.fori_loop(0, n_tiles, chunk, jnp.zeros(kn_ref.shape, F32))

    qt = qt_ref[0, 0]
    feat = lax.broadcasted_iota(jnp.int32, qt.shape, 0)
    zero = jnp.zeros_like(qt)
    q2_ref[:, :tq] = jnp.where(feat < HEAD_DIM, qt, zero)
    q2_ref[:, tq:] = jnp.where(feat >= HEAD_DIM, qt, zero)
    acc_ref[...] = jnp.zeros_like(acc_ref)
    p_buf[1] = jnp.zeros(p_buf.shape[1:], BF16)

    q2f = q2_ref[...].astype(F32)
    qn2 = jnp.sum(_tree8(q2f * q2f, jnp.add), axis=0, keepdims=True)
    kn2 = jnp.concatenate([kn_ref[...]] * (2 * tq // V_DIM), axis=1)
    bound = jnp.sqrt(qn2 * kn2) * ATTN_BOUND_SLACK
    b_ref[...] = bound
    bounded_ok = jnp.max(bound) <= ATTN_BOUND_MAX

    def cols(c):
        return slice(c * ATTN_CW, (c + 1) * ATTN_CW)

    def key_tile(j):
        return k_ref[0, pl.ds(pl.multiple_of(j * tq, tq), tq), :]

    def mask(c, s):
        q0 = (c * ATTN_CW) % tq
        return jnp.where(ck_ref[...] <= cq_ref[:, q0:q0 + ATTN_CW], s, NEG_INF)

    def pipeline(tile, last_pv, pairs):
        if pairs:
            odd = i % 2

            @pl.when(odd == 1)
            def _():
                tile(0, False)

            def body(jj, carry):
                tile(odd + 2 * jj, False)
                tile(odd + 2 * jj + 1, False)
                return carry

            lax.fori_loop(0, i // 2, body, 0)
        else:
            def body(j, carry):
                tile(j, False)
                return carry

            lax.fori_loop(0, i, body, 0)
        tile(i, True)
        last_pv(i, n_sub - 1)

    def qk(j, c):
        s_buf[c % 2] = jnp.dot(key_tile(j), q2_ref[:, cols(c)], preferred_element_type=F32)

    def stages(j, masked, pv_fn, softmax_fn):
        for c in range(n_sub):
            if c + 1 < n_sub:
                qk(j, c + 1)
            elif not masked:
                qk(j + 1, 0)
            if c > 0:
                pv_fn(j, c - 1)
            else:
                pv_fn(jnp.maximum(j - 1, 0), n_sub - 1)
            softmax_fn(c, masked)

    def b_softmax(c, masked):
        s = s_buf[c % 2]
        if masked:
            s = mask(c, s)
        p = jnp.exp2(s - b_ref[:, cols(c)])
        l8_ref[:, cols(c)] += _tree8(p, jnp.add)
        p_buf[c % 2] = p.astype(BF16)

    def b_pv(j, c):
        acc_ref[:, cols(c)] += jnp.dot(vt_ref[0, j], p_buf[c % 2], preferred_element_type=F32)

    @pl.when(bounded_ok)
    def _():
        l8_ref[...] = jnp.zeros_like(l8_ref)
        qk(0, 0)
        pipeline(functools.partial(stages, pv_fn=b_pv, softmax_fn=b_softmax), b_pv, pairs=True)
        l_ref[...] = jnp.sum(l8_ref[...], axis=0, keepdims=True)

    def pv(j, c):
        acc_ref[:, cols(c)] = (a_buf[c % 2] * acc_ref[:, cols(c)]
                               + jnp.dot(vt_ref[0, j], p_buf[c % 2],
                                         preferred_element_type=F32))

    def softmax(c, masked):
        s = s_buf[c % 2]
        if masked:
            s = mask(c, s)
        m_prev = m_ref[:, cols(c)]
        m_new = jnp.maximum(m_prev, jnp.max(_tree8(s, jnp.maximum), axis=0, keepdims=True))
        alpha = jnp.exp2(m_prev - m_new)
        p = jnp.exp2(s - m_new)
        l_ref[:, cols(c)] = alpha * l_ref[:, cols(c)] + jnp.sum(_tree8(p, jnp.add), axis=0,
                                                                keepdims=True)
        m_ref[:, cols(c)] = m_new
        a_buf[c % 2] = alpha
        p_buf[c % 2] = p.astype(BF16)

    @pl.when(jnp.logical_not(bounded_ok))
    def _():
        m_ref[...] = jnp.full(m_ref.shape, NEG_INF, F32)
        l_ref[...] = jnp.zeros_like(l_ref)
        a_buf[1] = jnp.zeros(a_buf.shape[1:], F32)
        qk(0, 0)
        pipeline(functools.partial(stages, pv_fn=pv, softmax_fn=softmax), pv, pairs=False)

    o = acc_ref[...] / l_ref[...]
    lv = lamv_ref[...]
    lam = (jnp.exp(jnp.sum(lv[0:1] * lv[1:2], axis=1, keepdims=True))
           - jnp.exp(jnp.sum(lv[2:3] * lv[3:4], axis=1, keepdims=True)) + lam_init)
    od = o[:, :tq] - lam * o[:, tq:]
    ms = jnp.mean(od * od, axis=0, keepdims=True)
    on = od * lax.rsqrt(ms + EPS) * (subg_ref[...] * (1.0 - lam_init))
    o_ref[...] = on.T.astype(BF16)


def _attention(qt, k, vt, cid_col, cid_row, lamv, subg_col, lam_init):
    _, nt, _, tq = qt.shape
    l = nt * tq
    return pl.pallas_call(
        functools.partial(_attn_kernel, tq=tq, lam_init=lam_init),
        out_shape=jax.ShapeDtypeStruct((l, N_HEADS * V_DIM), BF16),
        grid=(N_HEADS, nt),
        in_specs=[pl.BlockSpec((1, 1, V_DIM, tq), lambda h, i: (h, i, 0, 0)),
                  pl.BlockSpec((1, l, V_DIM), lambda h, i: (h, 0, 0)),
                  pl.BlockSpec((1, nt, V_DIM, tq), lambda h, i: (h, 0, 0, 0)),
                  pl.BlockSpec((1, tq), lambda h, i: (0, i)),
                  pl.BlockSpec((tq, 1), lambda h, i: (i, 0)),
                  pl.BlockSpec((4, HEAD_DIM), lambda h, i: (0, 0)),
                  pl.BlockSpec((V_DIM, 1), lambda h, i: (0, 0))],
        out_specs=pl.BlockSpec((tq, V_DIM), lambda h, i: (i, h)),
        scratch_shapes=[pltpu.VMEM((V_DIM, 2 * tq), BF16), pltpu.VMEM((1, 2 * tq), F32),
                        pltpu.VMEM((1, 2 * tq), F32), pltpu.VMEM((V_DIM, 2 * tq), F32),
                        pltpu.VMEM((2, tq, ATTN_CW), F32), pltpu.VMEM((2, tq, ATTN_CW), BF16),
                        pltpu.VMEM((2, 1, ATTN_CW), F32), pltpu.VMEM((1, V_DIM), F32),
                        pltpu.VMEM((1, 2 * tq), F32), pltpu.VMEM((8, 2 * tq), F32)],
        compiler_params=_cparams(("parallel", "arbitrary")),
        name="attn",
    )(qt, k, vt, cid_row, cid_col, lamv, subg_col)


SSM_OCT = 8
SSM_CB = 256


def _ssm_kernel(u_ref, kb_ref, bm_ref, cm_ref, pr_ref, pi_ref, y_ref, carry_ref, at_ref):
    slab = kb_ref.shape[2]

    @pl.when(pl.program_id(1) == 0)
    def _():
        carry_ref[...] = jnp.zeros_like(carry_ref)
        at_ref[...] = jnp.zeros_like(at_ref)
        for sg in range(SSM_T):
            for tau in range(sg, SSM_T):
                at_ref[sg * slab:(sg + 1) * slab, tau * slab:(tau + 1) * slab] = kb_ref[0, tau - sg]

    u = u_ref[0]
    cb = u.shape[0]
    half = carry_ref.shape[1] // 2
    pr, pi = pr_ref[0], pi_ref[0]

    def cmul(lvl, z):
        return pr[lvl:lvl + 1] * z + pi[lvl:lvl + 1] * pltpu.roll(z, half, 1)

    x = jnp.dot(u, bm_ref[0], preferred_element_type=F32)
    rows = lax.broadcasted_iota(jnp.int32, x.shape, 0)
    carry = carry_ref[...]
    x = x + jnp.where(rows == 0, cmul(0, carry), 0.0)

    def shift_down(z, d):
        if d % 8 == 0:
            return jnp.concatenate([jnp.zeros((d, z.shape[1]), F32), z[:cb - d]], axis=0)
        return jnp.where(rows >= d, pltpu.roll(z, d, 0), 0.0)

    for lvl in range(cb.bit_length() - 1):
        x = x + cmul(lvl, shift_down(x, 1 << lvl))
    carry_ref[...] = x[cb - 1:cb]
    xprev = shift_down(x, 1) + jnp.where(rows == 0, carry, 0.0)
    xb = xprev.astype(BF16)
    wn = 2 * slab
    for n in range(u.shape[1] // wn):
        kmax = (n + 1) * wn
        y = jnp.dot(u[:, :kmax], at_ref[:kmax, n * wn:(n + 1) * wn], preferred_element_type=F32)
        y = y + jnp.dot(xb, cm_ref[0, :, n * wn:(n + 1) * wn], preferred_element_type=F32)
        for h in range(wn // slab):
            tau = n * (wn // slab) + h
            y_ref[0, pl.ds(tau, cb, stride=SSM_T), :] = y[:, h * slab:(h + 1) * slab]


def _ssm(u_oct, at, bm, cm, pr, pi):
    n_oct, nb, w = u_oct.shape
    ns = bm.shape[2]
    per_oct = lambda a: pl.BlockSpec((1,) + a.shape[1:], lambda o, t: (o, 0, 0))
    return pl.pallas_call(
        _ssm_kernel,
        out_shape=jax.ShapeDtypeStruct((n_oct, nb * SSM_T, w // SSM_T), F32),
        grid=(n_oct, nb // SSM_CB),
        in_specs=[pl.BlockSpec((1, SSM_CB, w), lambda o, t: (o, t, 0)),
                  pl.BlockSpec((1,) + at.shape[1:], lambda o, t: (o, 0, 0, 0)),
                  per_oct(bm), per_oct(cm), per_oct(pr), per_oct(pi)],
        out_specs=pl.BlockSpec((1, SSM_CB * SSM_T, w // SSM_T), lambda o, t: (o, t, 0)),
        scratch_shapes=[pltpu.VMEM((1, ns), F32), pltpu.VMEM((w, w), BF16)],
        compiler_params=_cparams(("parallel", "arbitrary")),
        name="ssm",
    )(u_oct, at, bm, cm, pr, pi)


def _ssm_octets(kj, bm, cm, pr, pi):
    g = kj.shape[0]
    no, t, c, p = g // SSM_OCT, SSM_T, SSM_GROUP, SSM_STATE
    same = jnp.eye(SSM_OCT, dtype=bool)
    kj, bm, cm = kj.astype(BF16), bm.astype(BF16), cm.astype(BF16)
    zero = jnp.zeros((), BF16)
    at_o = kj.reshape(no, SSM_OCT, t, c, c).transpose(0, 2, 1, 4, 3)[:, :, :, :, None, :]
    at_o = jnp.where(same[None, None, :, None, :, None], at_o, zero)
    at_o = at_o.reshape(no, t, SSM_OCT * c, SSM_OCT * c)
    bm_o = bm.reshape(no, SSM_OCT, t, c, 2, p).transpose(0, 2, 1, 3, 4, 5)[:, :, :, :, :, None, :]
    bm_o = jnp.where(same[None, None, :, None, None, :, None], bm_o, zero)
    bm_o = bm_o.reshape(no, t * SSM_OCT * c, 2 * SSM_OCT * p)
    cm_o = cm.reshape(no, SSM_OCT, 2, p, t, c).transpose(0, 2, 1, 3, 4, 5)[:, :, :, :, :, None, :]
    cm_o = jnp.where(same[None, None, :, None, None, :, None], cm_o, zero)
    cm_o = cm_o.reshape(no, 2 * SSM_OCT * p, t * SSM_OCT * c)
    lanes = lambda a: (a.reshape(no, SSM_OCT, 16, 2, p).transpose(0, 2, 3, 1, 4)
                       .reshape(no, 16, 2 * SSM_OCT * p))
    return at_o, bm_o, cm_o, lanes(pr), lanes(pi)


def _ssm_prep(a_re, a_im, log_dt, b_re, b_im, c_re, c_im, d_skip):
    hp = lax.Precision.HIGHEST
    lr = jnp.minimum(a_re, -1e-4)
    li = a_im
    dt = jnp.exp(log_dt)[:, None]

    def lpow(n):
        n = jnp.asarray(n, F32)[..., None, None]
        mag = jnp.exp(n * (lr * dt))
        ang = n * (li * dt)
        return mag * jnp.cos(ang), mag * jnp.sin(ang)

    th = li * dt
    nr = jnp.expm1(lr * dt) * jnp.cos(th) - 2.0 * jnp.sin(0.5 * th) ** 2
    ni = jnp.exp(lr * dt) * jnp.sin(th)
    den = lr * lr + li * li
    fr = (nr * lr + ni * li) / den
    fi = (ni * lr - nr * li) / den
    bbr = fr[..., None] * b_re - fi[..., None] * b_im
    bbi = fr[..., None] * b_im + fi[..., None] * b_re

    t = jnp.arange(SSM_T)
    pjr, pji = lpow(t)
    mr = pjr[..., None] * bbr[None] - pji[..., None] * bbi[None]
    mi = pjr[..., None] * bbi[None] + pji[..., None] * bbr[None]
    kj = (jnp.einsum('gcp,jgpd->gjcd', c_re, mr, precision=hp)
          - jnp.einsum('gcp,jgpd->gjcd', c_im, mi, precision=hp))
    eye_c = jnp.eye(SSM_GROUP, dtype=F32)
    kj = kj.at[:, 0].add(eye_c[None] * d_skip[:, :, None])
    g = a_re.shape[0]

    rr, ri = lpow(SSM_T - 1 - t)
    sr = rr[..., None] * bbr[None] - ri[..., None] * bbi[None]
    si = rr[..., None] * bbi[None] + ri[..., None] * bbr[None]
    bm = jnp.concatenate([sr, si], axis=2)
    bm = bm.transpose(1, 0, 3, 2).reshape(g, SSM_W, 2 * SSM_STATE)

    qr, qi = lpow(t + 1)
    wr = c_re[None] * qr[:, :, None, :] - c_im[None] * qi[:, :, None, :]
    wi = c_re[None] * qi[:, :, None, :] + c_im[None] * qr[:, :, None, :]
    cm = jnp.concatenate([wr, -wi], axis=3)
    cm = cm.transpose(1, 3, 0, 2).reshape(g, 2 * SSM_STATE, SSM_W)

    er, ei = lpow(SSM_T * (2 ** jnp.arange(16)))
    pr = jnp.concatenate([er, er], axis=2).transpose(1, 0, 2)
    pi = jnp.concatenate([-ei, ei], axis=2).transpose(1, 0, 2)
    return kj, bm, cm, pr, pi


def _post_kernel(x_ref, attn_ref, y_ref, wglu_ref, bglu_ref, sn_ref, woa_ref, wob_ref,
                 g2_ref, o_ref):
    y = jnp.concatenate([y_ref[s] for s in range(y_ref.shape[0])], axis=1)
    y = jax.nn.gelu(y, approximate=True)
    z = jnp.dot(y.astype(BF16), wglu_ref[...], preferred_element_type=F32) + bglu_ref[...]
    y = y * _sigmoid(z)
    ms = jnp.mean(y * y, axis=-1, keepdims=True)
    y = y * lax.rsqrt(ms + EPS) * sn_ref[...]
    mixed = (jnp.dot(attn_ref[...], woa_ref[...], preferred_element_type=F32)
             + jnp.dot(y.astype(BF16), wob_ref[...], preferred_element_type=F32))
    o_ref[...] = x_ref[...] + g2_ref[...] * mixed


def _post(x, attn, y, w_glu, b_glu, sn, w_out, g2, tm=256):
    l, d = x.shape
    n_slab, _, slab = y.shape
    w = n_slab * slab
    row = lambda i: (i, 0)
    vec = lambda i: (0, 0)
    return pl.pallas_call(
        _post_kernel,
        out_shape=jax.ShapeDtypeStruct((l, d), F32),
        grid=(l // tm,),
        in_specs=[pl.BlockSpec((tm, d), row), pl.BlockSpec((tm, w), row),
                  pl.BlockSpec((n_slab, tm, slab), lambda i: (0, i, 0)),
                  pl.BlockSpec((w, w), vec), pl.BlockSpec((1, w), vec), pl.BlockSpec((1, w), vec),
                  pl.BlockSpec((w, d), lambda i: (0, 0)), pl.BlockSpec((w, d), lambda i: (1, 0)),
                  pl.BlockSpec((1, d), vec)],
        out_specs=pl.BlockSpec((tm, d), row),
        compiler_params=_cparams(("parallel",)),
        name="post",
    )(x, attn, y, w_glu, b_glu, sn, w_out, w_out, g2)


def _rope_tables(positions):
    inv_freq = ROPE_THETA ** (-jnp.arange(0, ROT_DIM, 2, dtype=F32) / ROT_DIM)
    half = ROT_DIM // 2
    dim = jnp.arange(V_DIM) % HEAD_DIM
    ang = positions.astype(F32)[:, None] * inv_freq[dim % half][None, :]
    cos, sin = jnp.cos(ang), jnp.sin(ang)
    first, second = dim < half, (dim >= half) & (dim < ROT_DIM)
    cos_t = jnp.where(first | second, cos, 1.0)
    s1_t = jnp.where(first, -sin, 0.0)
    s2_t = jnp.where(second, sin, 0.0)
    return cos_t, s1_t, s2_t


def kernel(x, c, positions, w_ada, b_ada, ffn1_norm, ffn1_w1, ffn1_w3, ffn1_w2, mix_norm, w_in, q_norm, k_norm, lambda_q1, lambda_k1, lambda_q2, lambda_k2, attn_subln, ssm_a_re, ssm_a_im, ssm_log_dt, ssm_b_re, ssm_b_im, ssm_c_re, ssm_c_im, ssm_d, w_glu, b_glu, ssm_out_norm, w_out, ffn2_norm, ffn2_w1, ffn2_w3, ffn2_w2):
    batch, seq, d = x.shape
    depth = w_ada.shape[0]
    assert batch == 1 and seq % (SSM_T * SSM_CB) == 0
    pos = positions[0]
    cid = pos // CHUNK
    cid_col, cid_row = cid.reshape(seq, 1), cid.reshape(1, seq)
    cos_t, s1_t, s2_t = _rope_tables(pos)
    lane = jnp.arange(V_DIM)
    gsum = (lane[:, None] // HEAD_DIM == lane[None, :] // HEAD_DIM).astype(BF16)
    tile2 = lambda a: jnp.concatenate([a, a]).reshape(1, V_DIM)
    nb = seq // SSM_T
    n_groups = ssm_a_re.shape[1]

    xs = x[0]
    for l in range(depth):
        lam_init = 0.8 - 0.6 * math.exp(-0.3 * l)
        mod = _ada(c, w_ada[l], b_ada[l])
        sh1, sc1, g1, sh2, sc2, g2, sh3, sc3, g3 = jnp.split(mod, 9, axis=-1)
        vec = lambda a: a.reshape(1, -1)

        xs = _ffn(xs, vec(ffn1_norm[l]), sh1, sc1, g1, ffn1_w1[l], ffn1_w3[l], ffn1_w2[l],
                  skewed=False)

        qt, k, vt, u = _proj(xs, vec(mix_norm[l]), sh2, sc2, w_in[l].astype(BF16),
                             tile2(q_norm[l]), tile2(k_norm[l]), cos_t, s1_t, s2_t, gsum)

        lamv = jnp.stack([lambda_q1[l], lambda_k1[l], lambda_q2[l], lambda_k2[l]]).astype(F32)
        attn = _attention(qt, k, vt, cid_col, cid_row, lamv,
                          attn_subln[l].astype(F32).reshape(V_DIM, 1), lam_init)

        at, bm, cm, pr, pi = _ssm_prep(ssm_a_re[l], ssm_a_im[l], ssm_log_dt[l], ssm_b_re[l],
                                       ssm_b_im[l], ssm_c_re[l], ssm_c_im[l], ssm_d[l])
        n_oct, slab = n_groups // SSM_OCT, SSM_OCT * SSM_GROUP
        u_oct = (u.reshape(nb, SSM_T, n_oct, slab).transpose(2, 0, 1, 3)
                 .reshape(n_oct, nb, SSM_T * slab))
        y = _ssm(u_oct, *_ssm_octets(at, bm, cm, pr, pi))

        xs = _post(xs, attn, y, w_glu[l].astype(BF16), vec(b_glu[l]), vec(ssm_out_norm[l]),
                   w_out[l].astype(BF16), g2)

        xs = _ffn(xs, vec(ffn2_norm[l]), sh3, sc3, g3, ffn2_w1[l], ffn2_w3[l], ffn2_w2[l],
                  skewed=True)
    return xs[None]
```

```python
import functools
import math

import jax
import jax.numpy as jnp
from jax import lax
from jax.experimental import pallas as pl
from jax.experimental.pallas import tpu as pltpu

F32 = jnp.float32
BF16 = jnp.bfloat16

CHUNK = 64
N_HEADS = 8
HEAD_DIM = 64
V_DIM = 128
ROT_DIM = 16
ROPE_THETA = 500000.0
SSM_GROUP = 16
SSM_STATE = 64
EPS = 1e-6
NEG_INF = -1e30
LOG2E = 1.4426950408889634

SSM_T = 16
SSM_W = SSM_T * SSM_GROUP

VMEM_LIMIT = 56 * 1024 * 1024


def _cparams(sem):
    return pltpu.CompilerParams(dimension_semantics=sem, vmem_limit_bytes=VMEM_LIMIT)


def _sigmoid(x):
    return 1.0 / (1.0 + jnp.exp(-x))


def _norm_mod(x, g, shift, scale):
    ms = jnp.mean(x * x, axis=-1, keepdims=True)
    y = x * lax.rsqrt(ms + EPS) * g
    return y * (1.0 + scale) + shift


def _ada_kernel(c_ref, w_ref, b_ref, o_ref):
    c = c_ref[...]
    cond = c * _sigmoid(c)
    o_ref[...] = jnp.dot(cond, w_ref[...], preferred_element_type=F32,
                         precision=lax.Precision.HIGHEST) + b_ref[...]


def _ada(c, w_ada, b_ada, tn=1024):
    d, n = w_ada.shape
    c8 = jnp.broadcast_to(c, (8, d))
    out = pl.pallas_call(
        _ada_kernel,
        out_shape=jax.ShapeDtypeStruct((8, n), F32),
        grid=(n // tn,),
        in_specs=[pl.BlockSpec((8, d), lambda j: (0, 0)),
                  pl.BlockSpec((d, tn), lambda j: (0, j)),
                  pl.BlockSpec((1, tn), lambda j: (0, j))],
        out_specs=pl.BlockSpec((8, tn), lambda j: (0, j)),
        compiler_params=_cparams(("arbitrary",)),
        name="ada",
    )(c8, w_ada, b_ada.reshape(1, n))
    return out[0:1]


def _ffn_chunk_kernel(x_ref, g_ref, sh_ref, sc_ref, gate_ref, w1_ref, w3_ref, w2_ref,
                      o_ref, h_ref, acc_ref):
    f = pl.program_id(1)

    @pl.when(f == 0)
    def _():
        h = _norm_mod(x_ref[...], g_ref[...], sh_ref[...], sc_ref[...])
        h_ref[...] = h.astype(BF16)
        acc_ref[...] = jnp.zeros_like(acc_ref)

    h = h_ref[...]
    a = jnp.dot(h, w1_ref[...], preferred_element_type=F32)
    b = jnp.dot(h, w3_ref[...], preferred_element_type=F32)
    g = (a * _sigmoid(a)) * b
    acc_ref[...] += jnp.dot(g.astype(BF16), w2_ref[...], preferred_element_type=F32)

    @pl.when(f == pl.num_programs(1) - 1)
    def _():
        o_ref[...] = x_ref[...] + (0.5 * gate_ref[...]) * acc_ref[...]


def _ffn(x, g, shift, scale, gate, w1, w3, w2, tm=512, tf=512):
    l, d = x.shape
    row = lambda i, f: (i, 0)
    vec = lambda i, f: (0, 0)
    return pl.pallas_call(
        _ffn_chunk_kernel,
        out_shape=jax.ShapeDtypeStruct((l, d), F32),
        grid=(l // tm, w1.shape[1] // tf),
        in_specs=[pl.BlockSpec((tm, d), row),
                  pl.BlockSpec((1, d), vec), pl.BlockSpec((1, d), vec),
                  pl.BlockSpec((1, d), vec), pl.BlockSpec((1, d), vec),
                  pl.BlockSpec((d, tf), lambda i, f: (0, f)),
                  pl.BlockSpec((d, tf), lambda i, f: (0, f)),
                  pl.BlockSpec((tf, d), lambda i, f: (f, 0))],
        out_specs=pl.BlockSpec((tm, d), row),
        scratch_shapes=[pltpu.VMEM((tm, d), BF16), pltpu.VMEM((tm, d), F32)],
        compiler_params=_cparams(("parallel", "arbitrary")),
        name="ffn",
    )(x, g, shift, scale, gate, w1.astype(BF16), w3.astype(BF16), w2.astype(BF16))


def _proj_kernel(x_ref, g_ref, sh_ref, sc_ref, w_ref, qn_ref, kn_ref, cos_ref, s1_ref,
                 s2_ref, gsum_ref, qt_ref, k_ref, vt_ref, u_ref, h_ref, *, q_scale):
    j = pl.program_id(1)

    @pl.when(j == 0)
    def _():
        h = _norm_mod(x_ref[...], g_ref[...], sh_ref[...], sc_ref[...])
        h_ref[...] = h.astype(BF16)

    p = jnp.dot(h_ref[...], w_ref[...], preferred_element_type=F32)

    def qk_rot(gain_ref, scale, hd):
        pc = p[:, hd * V_DIM:(hd + 1) * V_DIM]
        ss = jnp.dot((pc * pc).astype(BF16), gsum_ref[...], preferred_element_type=F32)
        y = pc * lax.rsqrt(ss * (1.0 / HEAD_DIM) + EPS) * (gain_ref[...] * scale)
        return (y * cos_ref[...] + pltpu.roll(y, V_DIM - ROT_DIM // 2, 1) * s1_ref[...]
                + pltpu.roll(y, ROT_DIM // 2, 1) * s2_ref[...])

    @pl.when(j == 0)
    def _():
        for hd in range(N_HEADS):
            qt_ref[hd, 0] = qk_rot(qn_ref, q_scale, hd).T.astype(BF16)

    @pl.when(j == 1)
    def _():
        for hd in range(N_HEADS):
            k_ref[hd] = qk_rot(kn_ref, 1.0, hd).astype(BF16)

    @pl.when(j == 2)
    def _():
        for hd in range(N_HEADS):
            vt_ref[hd, 0] = p[:, hd * V_DIM:(hd + 1) * V_DIM].T.astype(BF16)

    @pl.when(j == 3)
    def _():
        u_ref[...] = p.astype(BF16)


def _proj(x, g, shift, scale, w_in, qn, kn, cos_t, s1_t, s2_t, gsum, tm=512):
    l, d = x.shape
    tn = 1024
    row = lambda i, j: (i, 0)
    vec = lambda i, j: (0, 0)
    hm = lambda i, j: (0, i, 0)
    tm_t = lambda i, j: (0, i, 0, 0)
    head_major = jax.ShapeDtypeStruct((N_HEADS, l, V_DIM), BF16)
    tiled_t = jax.ShapeDtypeStruct((N_HEADS, l // tm, V_DIM, tm), BF16)
    return pl.pallas_call(
        functools.partial(_proj_kernel, q_scale=HEAD_DIM ** -0.5 * LOG2E),
        out_shape=(tiled_t, head_major, tiled_t, jax.ShapeDtypeStruct((l, tn), BF16)),
        grid=(l // tm, 4),
        in_specs=[pl.BlockSpec((tm, d), row),
                  pl.BlockSpec((1, d), vec), pl.BlockSpec((1, d), vec), pl.BlockSpec((1, d), vec),
                  pl.BlockSpec((d, tn), lambda i, j: (0, j)),
                  pl.BlockSpec((1, V_DIM), vec), pl.BlockSpec((1, V_DIM), vec),
                  pl.BlockSpec((tm, V_DIM), row), pl.BlockSpec((tm, V_DIM), row),
                  pl.BlockSpec((tm, V_DIM), row),
                  pl.BlockSpec((V_DIM, V_DIM), vec)],
        out_specs=(pl.BlockSpec((N_HEADS, 1, V_DIM, tm), tm_t), pl.BlockSpec((N_HEADS, tm, V_DIM), hm),
                   pl.BlockSpec((N_HEADS, 1, V_DIM, tm), tm_t), pl.BlockSpec((tm, tn), row)),
        scratch_shapes=[pltpu.VMEM((tm, d), BF16)],
        compiler_params=_cparams(("parallel", "arbitrary")),
        name="proj",
    )(x, g, shift, scale, w_in, qn, kn, cos_t, s1_t, s2_t, gsum)


ATTN_CW = 256
ATTN_GROUP = 4
ATTN_BOUND_SLACK = 1.02
ATTN_BOUND_MAX = 48.0


def _tree8(x, op):
    while x.shape[0] > 8:
        h = x.shape[0] // 2
        x = op(x[:h], x[h:])
    return x


def _attn_kernel(qt_ref, k_ref, vt_ref, cq_ref, ck_ref, lamv_ref, subg_ref, o_ref,
                 q2_ref, m_ref, l_ref, acc_ref, s_buf, p_buf, a_buf, kn_ref, b_ref, l8_ref,
                 *, tq, lam_init):
    i = pl.program_id(1)
    n_sub = 2 * tq // ATTN_CW
    n_tiles = k_ref.shape[1] // tq

    @pl.when(i == 0)
    def _():
        ones = jnp.ones((V_DIM, V_DIM), BF16)

        def chunk(r, mx):
            kk = k_ref[0, pl.ds(pl.multiple_of(r * tq, tq), tq), :].astype(F32)
            n2 = jnp.dot((kk * kk).astype(BF16), ones, preferred_element_type=F32)
            return jnp.maximum(mx, jnp.max(_tree8(n2, jnp.maximum), axis=0, keepdims=True))

        kn_ref[...] = lax.fori_loop(0, n_tiles, chunk, jnp.zeros(kn_ref.shape, F32))

    qt = qt_ref[0, 0]
    feat = lax.broadcasted_iota(jnp.int32, qt.shape, 0)
    zero = jnp.zeros_like(qt)
    q2_ref[:, :tq] = jnp.where(feat < HEAD_DIM, qt, zero)
    q2_ref[:, tq:] = jnp.where(feat >= HEAD_DIM, qt, zero)
    acc_ref[...] = jnp.zeros_like(acc_ref)
    p_buf[1] = jnp.zeros(p_buf.shape[1:], BF16)

    q2f = q2_ref[...].astype(F32)
    qn2 = jnp.sum(_tree8(q2f * q2f, jnp.add), axis=0, keepdims=True)
    kn2 = jnp.concatenate([kn_ref[...]] * (2 * tq // V_DIM), axis=1)
    bound = jnp.sqrt(qn2 * kn2) * ATTN_BOUND_SLACK
    b_ref[...] = bound
    bounded_ok = jnp.max(bound) <= ATTN_BOUND_MAX

    def cols(c):
        return slice(c * ATTN_CW, (c + 1) * ATTN_CW)

    def key_tile(j):
        return k_ref[0, pl.ds(pl.multiple_of(j * tq, tq), tq), :]

    def mask(c, s):
        q0 = (c * ATTN_CW) % tq
        return jnp.where(ck_ref[...] <= cq_ref[:, q0:q0 + ATTN_CW], s, NEG_INF)

    def pipeline(tile, last_pv, group):
        lead = i % group if group > 1 else i

        def single(j, carry):
            tile(j, False)
            return carry

        def grouped(jj, carry):
            for t in range(group):
                tile(lead + group * jj + t, False)
            return carry

        lax.fori_loop(0, lead, single, 0)
        if group > 1:
            lax.fori_loop(0, i // group, grouped, 0)
        tile(i, True)
        last_pv(i, n_sub - 1)

    def qk(j, c):
        s_buf[c % 2] = jnp.dot(key_tile(j), q2_ref[:, cols(c)], preferred_element_type=F32)

    def stages(j, masked, pv_fn, softmax_fn):
        for c in range(n_sub):
            if c + 1 < n_sub:
                qk(j, c + 1)
            elif not masked:
                qk(j + 1, 0)
            if c > 0:
                pv_fn(j, c - 1)
            else:
                pv_fn(jnp.maximum(j - 1, 0), n_sub - 1)
            softmax_fn(c, masked)

    def b_softmax(c, masked):
        s = s_buf[c % 2]
        if masked:
            s = mask(c, s)
        p = jnp.exp2(s - b_ref[:, cols(c)])
        l8_ref[:, cols(c)] += _tree8(p, jnp.add)
        p_buf[c % 2] = p.astype(BF16)

    def b_pv(j, c):
        acc_ref[:, cols(c)] += jnp.dot(vt_ref[0, j], p_buf[c % 2], preferred_element_type=F32)

    @pl.when(bounded_ok)
    def _():
        l8_ref[...] = jnp.zeros_like(l8_ref)
        qk(0, 0)
        pipeline(functools.partial(stages, pv_fn=b_pv, softmax_fn=b_softmax), b_pv,
                 group=ATTN_GROUP)
        l_ref[...] = jnp.sum(l8_ref[...], axis=0, keepdims=True)

    def pv(j, c):
        acc_ref[:, cols(c)] = (a_buf[c % 2] * acc_ref[:, cols(c)]
                               + jnp.dot(vt_ref[0, j], p_buf[c % 2],
                                         preferred_element_type=F32))

    def softmax(c, masked):
        s = s_buf[c % 2]
        if masked:
            s = mask(c, s)
        m_prev = m_ref[:, cols(c)]
        m_new = jnp.maximum(m_prev, jnp.max(_tree8(s, jnp.maximum), axis=0, keepdims=True))
        alpha = jnp.exp2(m_prev - m_new)
        p = jnp.exp2(s - m_new)
        l_ref[:, cols(c)] = alpha * l_ref[:, cols(c)] + jnp.sum(_tree8(p, jnp.add), axis=0,
                                                                keepdims=True)
        m_ref[:, cols(c)] = m_new
        a_buf[c % 2] = alpha
        p_buf[c % 2] = p.astype(BF16)

    @pl.when(jnp.logical_not(bounded_ok))
    def _():
        m_ref[...] = jnp.full(m_ref.shape, NEG_INF, F32)
        l_ref[...] = jnp.zeros_like(l_ref)
        a_buf[1] = jnp.zeros(a_buf.shape[1:], F32)
        qk(0, 0)
        pipeline(functools.partial(stages, pv_fn=pv, softmax_fn=softmax), pv, group=1)

    o = acc_ref[...] / l_ref[...]
    lv = lamv_ref[...]
    lam = (jnp.exp(jnp.sum(lv[0:1] * lv[1:2], axis=1, keepdims=True))
           - jnp.exp(jnp.sum(lv[2:3] * lv[3:4], axis=1, keepdims=True)) + lam_init)
    od = o[:, :tq] - lam * o[:, tq:]
    ms = jnp.mean(od * od, axis=0, keepdims=True)
    on = od * lax.rsqrt(ms + EPS) * (subg_ref[...] * (1.0 - lam_init))
    o_ref[...] = on.T.astype(BF16)


def _attention(qt, k, vt, cid_col, cid_row, lamv, subg_col, lam_init):
    _, nt, _, tq = qt.shape
    l = nt * tq
    return pl.pallas_call(
        functools.partial(_attn_kernel, tq=tq, lam_init=lam_init),
        out_shape=jax.ShapeDtypeStruct((l, N_HEADS * V_DIM), BF16),
        grid=(N_HEADS, nt),
        in_specs=[pl.BlockSpec((1, 1, V_DIM, tq), lambda h, i: (h, i, 0, 0)),
                  pl.BlockSpec((1, l, V_DIM), lambda h, i: (h, 0, 0)),
                  pl.BlockSpec((1, nt, V_DIM, tq), lambda h, i: (h, 0, 0, 0)),
                  pl.BlockSpec((1, tq), lambda h, i: (0, i)),
                  pl.BlockSpec((tq, 1), lambda h, i: (i, 0)),
                  pl.BlockSpec((4, HEAD_DIM), lambda h, i: (0, 0)),
                  pl.BlockSpec((V_DIM, 1), lambda h, i: (0, 0))],
        out_specs=pl.BlockSpec((tq, V_DIM), lambda h, i: (i, h)),
        scratch_shapes=[pltpu.VMEM((V_DIM, 2 * tq), BF16), pltpu.VMEM((1, 2 * tq), F32),
                        pltpu.VMEM((1, 2 * tq), F32), pltpu.VMEM((V_DIM, 2 * tq), F32),
                        pltpu.VMEM((2, tq, ATTN_CW), F32), pltpu.VMEM((2, tq, ATTN_CW), BF16),
                        pltpu.VMEM((2, 1, ATTN_CW), F32), pltpu.VMEM((1, V_DIM), F32),
                        pltpu.VMEM((1, 2 * tq), F32), pltpu.VMEM((8, 2 * tq), F32)],
        compiler_params=_cparams(("parallel", "arbitrary")),
        name="attn",
    )(qt, k, vt, cid_row, cid_col, lamv, subg_col)


SSM_OCT = 8
SSM_CB = 256


def _ssm_kernel(u_ref, kb_ref, bm_ref, cm_ref, pr_ref, pi_ref, y_ref, carry_ref, at_ref):
    slab = kb_ref.shape[2]

    @pl.when(pl.program_id(1) == 0)
    def _():
        carry_ref[...] = jnp.zeros_like(carry_ref)
        at_ref[...] = jnp.zeros_like(at_ref)
        for sg in range(SSM_T):
            for tau in range(sg, SSM_T):
                at_ref[sg * slab:(sg + 1) * slab, tau * slab:(tau + 1) * slab] = kb_ref[0, tau - sg]

    u = u_ref[0]
    cb = u.shape[0]
    half = carry_ref.shape[1] // 2
    pr, pi = pr_ref[0], pi_ref[0]

    def cmul(lvl, z):
        return pr[lvl:lvl + 1] * z + pi[lvl:lvl + 1] * pltpu.roll(z, half, 1)

    x = jnp.dot(u, bm_ref[0], preferred_element_type=F32)
    rows = lax.broadcasted_iota(jnp.int32, x.shape, 0)
    carry = carry_ref[...]
    x = x + jnp.where(rows == 0, cmul(0, carry), 0.0)

    def shift_down(z, d):
        if d % 8 == 0:
            return jnp.concatenate([jnp.zeros((d, z.shape[1]), F32), z[:cb - d]], axis=0)
        return jnp.where(rows >= d, pltpu.roll(z, d, 0), 0.0)

    for lvl in range(cb.bit_length() - 1):
        x = x + cmul(lvl, shift_down(x, 1 << lvl))
    carry_ref[...] = x[cb - 1:cb]
    xprev = shift_down(x, 1) + jnp.where(rows == 0, carry, 0.0)
    xb = xprev.astype(BF16)
    wn = 2 * slab
    for n in range(u.shape[1] // wn):
        kmax = (n + 1) * wn
        y = jnp.dot(u[:, :kmax], at_ref[:kmax, n * wn:(n + 1) * wn], preferred_element_type=F32)
        y = y + jnp.dot(xb, cm_ref[0, :, n * wn:(n + 1) * wn], preferred_element_type=F32)
        for h in range(wn // slab):
            tau = n * (wn // slab) + h
            y_ref[0, pl.ds(tau, cb, stride=SSM_T), :] = y[:, h * slab:(h + 1) * slab]


def _ssm(u_oct, at, bm, cm, pr, pi):
    n_oct, nb, w = u_oct.shape
    ns = bm.shape[2]
    per_oct = lambda a: pl.BlockSpec((1,) + a.shape[1:], lambda o, t: (o, 0, 0))
    return pl.pallas_call(
        _ssm_kernel,
        out_shape=jax.ShapeDtypeStruct((n_oct, nb * SSM_T, w // SSM_T), F32),
        grid=(n_oct, nb // SSM_CB),
        in_specs=[pl.BlockSpec((1, SSM_CB, w), lambda o, t: (o, t, 0)),
                  pl.BlockSpec((1,) + at.shape[1:], lambda o, t: (o, 0, 0, 0)),
                  per_oct(bm), per_oct(cm), per_oct(pr), per_oct(pi)],
        out_specs=pl.BlockSpec((1, SSM_CB * SSM_T, w // SSM_T), lambda o, t: (o, t, 0)),
        scratch_shapes=[pltpu.VMEM((1, ns), F32), pltpu.VMEM((w, w), BF16)],
        compiler_params=_cparams(("parallel", "arbitrary")),
        name="ssm",
    )(u_oct, at, bm, cm, pr, pi)


def _ssm_octets(kj, bm, cm, pr, pi):
    g = kj.shape[0]
    no, t, c, p = g // SSM_OCT, SSM_T, SSM_GROUP, SSM_STATE
    same = jnp.eye(SSM_OCT, dtype=bool)
    kj, bm, cm = kj.astype(BF16), bm.astype(BF16), cm.astype(BF16)
    zero = jnp.zeros((), BF16)
    at_o = kj.reshape(no, SSM_OCT, t, c, c).transpose(0, 2, 1, 4, 3)[:, :, :, :, None, :]
    at_o = jnp.where(same[None, None, :, None, :, None], at_o, zero)
    at_o = at_o.reshape(no, t, SSM_OCT * c, SSM_OCT * c)
    bm_o = bm.reshape(no, SSM_OCT, t, c, 2, p).transpose(0, 2, 1, 3, 4, 5)[:, :, :, :, :, None, :]
    bm_o = jnp.where(same[None, None, :, None, None, :, None], bm_o, zero)
    bm_o = bm_o.reshape(no, t * SSM_OCT * c, 2 * SSM_OCT * p)
    cm_o = cm.reshape(no, SSM_OCT, 2, p, t, c).transpose(0, 2, 1, 3, 4, 5)[:, :, :, :, :, None, :]
    cm_o = jnp.where(same[None, None, :, None, None, :, None], cm_o, zero)
    cm_o = cm_o.reshape(no, 2 * SSM_OCT * p, t * SSM_OCT * c)
    lanes = lambda a: (a.reshape(no, SSM_OCT, 16, 2, p).transpose(0, 2, 3, 1, 4)
                       .reshape(no, 16, 2 * SSM_OCT * p))
    return at_o, bm_o, cm_o, lanes(pr), lanes(pi)


def _ssm_prep(a_re, a_im, log_dt, b_re, b_im, c_re, c_im, d_skip):
    hp = lax.Precision.HIGHEST
    lr = jnp.minimum(a_re, -1e-4)
    li = a_im
    dt = jnp.exp(log_dt)[:, None]

    def lpow(n):
        n = jnp.asarray(n, F32)[..., None, None]
        mag = jnp.exp(n * (lr * dt))
        ang = n * (li * dt)
        return mag * jnp.cos(ang), mag * jnp.sin(ang)

    th = li * dt
    nr = jnp.expm1(lr * dt) * jnp.cos(th) - 2.0 * jnp.sin(0.5 * th) ** 2
    ni = jnp.exp(lr * dt) * jnp.sin(th)
    den = lr * lr + li * li
    fr = (nr * lr + ni * li) / den
    fi = (ni * lr - nr * li) / den
    bbr = fr[..., None] * b_re - fi[..., None] * b_im
    bbi = fr[..., None] * b_im + fi[..., None] * b_re

    t = jnp.arange(SSM_T)
    pjr, pji = lpow(t)
    mr = pjr[..., None] * bbr[None] - pji[..., None] * bbi[None]
    mi = pjr[..., None] * bbi[None] + pji[..., None] * bbr[None]
    kj = (jnp.einsum('gcp,jgpd->gjcd', c_re, mr, precision=hp)
          - jnp.einsum('gcp,jgpd->gjcd', c_im, mi, precision=hp))
    eye_c = jnp.eye(SSM_GROUP, dtype=F32)
    kj = kj.at[:, 0].add(eye_c[None] * d_skip[:, :, None])
    g = a_re.shape[0]

    rr, ri = lpow(SSM_T - 1 - t)
    sr = rr[..., None] * bbr[None] - ri[..., None] * bbi[None]
    si = rr[..., None] * bbi[None] + ri[..., None] * bbr[None]
    bm = jnp.concatenate([sr, si], axis=2)
    bm = bm.transpose(1, 0, 3, 2).reshape(g, SSM_W, 2 * SSM_STATE)

    qr, qi = lpow(t + 1)
    wr = c_re[None] * qr[:, :, None, :] - c_im[None] * qi[:, :, None, :]
    wi = c_re[None] * qi[:, :, None, :] + c_im[None] * qr[:, :, None, :]
    cm = jnp.concatenate([wr, -wi], axis=3)
    cm = cm.transpose(1, 3, 0, 2).reshape(g, 2 * SSM_STATE, SSM_W)

    er, ei = lpow(SSM_T * (2 ** jnp.arange(16)))
    pr = jnp.concatenate([er, er], axis=2).transpose(1, 0, 2)
    pi = jnp.concatenate([-ei, ei], axis=2).transpose(1, 0, 2)
    return kj, bm, cm, pr, pi


def _post_kernel(x_ref, attn_ref, y_ref, wglu_ref, bglu_ref, sn_ref, woa_ref, wob_ref,
                 g2_ref, o_ref):
    y = jnp.concatenate([y_ref[s] for s in range(y_ref.shape[0])], axis=1)
    y = jax.nn.gelu(y, approximate=True)
    z = jnp.dot(y.astype(BF16), wglu_ref[...], preferred_element_type=F32) + bglu_ref[...]
    y = y * _sigmoid(z)
    ms = jnp.mean(y * y, axis=-1, keepdims=True)
    y = y * lax.rsqrt(ms + EPS) * sn_ref[...]
    mixed = (jnp.dot(attn_ref[...], woa_ref[...], preferred_element_type=F32)
             + jnp.dot(y.astype(BF16), wob_ref[...], preferred_element_type=F32))
    o_ref[...] = x_ref[...] + g2_ref[...] * mixed


def _post(x, attn, y, w_glu, b_glu, sn, w_out, g2, tm=256):
    l, d = x.shape
    n_slab, _, slab = y.shape
    w = n_slab * slab
    row = lambda i: (i, 0)
    vec = lambda i: (0, 0)
    return pl.pallas_call(
        _post_kernel,
        out_shape=jax.ShapeDtypeStruct((l, d), F32),
        grid=(l // tm,),
        in_specs=[pl.BlockSpec((tm, d), row), pl.BlockSpec((tm, w), row),
                  pl.BlockSpec((n_slab, tm, slab), lambda i: (0, i, 0)),
                  pl.BlockSpec((w, w), vec), pl.BlockSpec((1, w), vec), pl.BlockSpec((1, w), vec),
                  pl.BlockSpec((w, d), lambda i: (0, 0)), pl.BlockSpec((w, d), lambda i: (1, 0)),
                  pl.BlockSpec((1, d), vec)],
        out_specs=pl.BlockSpec((tm, d), row),
        compiler_params=_cparams(("parallel",)),
        name="post",
    )(x, attn, y, w_glu, b_glu, sn, w_out, w_out, g2)


def _rope_tables(positions):
    inv_freq = ROPE_THETA ** (-jnp.arange(0, ROT_DIM, 2, dtype=F32) / ROT_DIM)
    half = ROT_DIM // 2
    dim = jnp.arange(V_DIM) % HEAD_DIM
    ang = positions.astype(F32)[:, None] * inv_freq[dim % half][None, :]
    cos, sin = jnp.cos(ang), jnp.sin(ang)
    first, second = dim < half, (dim >= half) & (dim < ROT_DIM)
    cos_t = jnp.where(first | second, cos, 1.0)
    s1_t = jnp.where(first, -sin, 0.0)
    s2_t = jnp.where(second, sin, 0.0)
    return cos_t, s1_t, s2_t


def kernel(x, c, positions, w_ada, b_ada, ffn1_norm, ffn1_w1, ffn1_w3, ffn1_w2, mix_norm, w_in, q_norm, k_norm, lambda_q1, lambda_k1, lambda_q2, lambda_k2, attn_subln, ssm_a_re, ssm_a_im, ssm_log_dt, ssm_b_re, ssm_b_im, ssm_c_re, ssm_c_im, ssm_d, w_glu, b_glu, ssm_out_norm, w_out, ffn2_norm, ffn2_w1, ffn2_w3, ffn2_w2):
    batch, seq, d = x.shape
    depth = w_ada.shape[0]
    assert batch == 1 and seq % (SSM_T * SSM_CB) == 0
    pos = positions[0]
    cid = pos // CHUNK
    cid_col, cid_row = cid.reshape(seq, 1), cid.reshape(1, seq)
    cos_t, s1_t, s2_t = _rope_tables(pos)
    lane = jnp.arange(V_DIM)
    gsum = (lane[:, None] // HEAD_DIM == lane[None, :] // HEAD_DIM).astype(BF16)
    tile2 = lambda a: jnp.concatenate([a, a]).reshape(1, V_DIM)
    nb = seq // SSM_T
    n_groups = ssm_a_re.shape[1]

    xs = x[0]
    for l in range(depth):
        lam_init = 0.8 - 0.6 * math.exp(-0.3 * l)
        mod = _ada(c, w_ada[l], b_ada[l])
        sh1, sc1, g1, sh2, sc2, g2, sh3, sc3, g3 = jnp.split(mod, 9, axis=-1)
        vec = lambda a: a.reshape(1, -1)

        xs = _ffn(xs, vec(ffn1_norm[l]), sh1, sc1, g1, ffn1_w1[l], ffn1_w3[l], ffn1_w2[l])

        qt, k, vt, u = _proj(xs, vec(mix_norm[l]), sh2, sc2, w_in[l].astype(BF16),
                             tile2(q_norm[l]), tile2(k_norm[l]), cos_t, s1_t, s2_t, gsum)

        lamv = jnp.stack([lambda_q1[l], lambda_k1[l], lambda_q2[l], lambda_k2[l]]).astype(F32)
        attn = _attention(qt, k, vt, cid_col, cid_row, lamv,
                          attn_subln[l].astype(F32).reshape(V_DIM, 1), lam_init)

        at, bm, cm, pr, pi = _ssm_prep(ssm_a_re[l], ssm_a_im[l], ssm_log_dt[l], ssm_b_re[l],
                                       ssm_b_im[l], ssm_c_re[l], ssm_c_im[l], ssm_d[l])
        n_oct, slab = n_groups // SSM_OCT, SSM_OCT * SSM_GROUP
        u_oct = (u.reshape(nb, SSM_T, n_oct, slab).transpose(2, 0, 1, 3)
                 .reshape(n_oct, nb, SSM_T * slab))
        y = _ssm(u_oct, *_ssm_octets(at, bm, cm, pr, pi))

        xs = _post(xs, attn, y, w_glu[l].astype(BF16), vec(b_glu[l]), vec(ssm_out_norm[l]),
                   w_out[l].astype(BF16), g2)

        xs = _ffn(xs, vec(ffn2_norm[l]), sh3, sc3, g3, ffn2_w1[l], ffn2_w3[l], ffn2_w2[l])
    return xs[None]
```
